```python
import math
import jax
import jax.numpy as jnp
from jax import lax
import numpy as np

D_MODEL = 4096
BATCH = 8
SEQ = 2048
DEPTH = 1
DEC_BATCH = 32
DEC_SEQ = 64
PAST_LEN = 1024

CHUNK = 64
QBLK = 128
HEAD_DIM = 128
N_HEADS_A = D_MODEL // (2 * HEAD_DIM)
N_KV_A = 4
IDX_HEADS = 16
IDX_DIM = 64
TOPK_MAX = 256
N_HEADS_B = D_MODEL // (2 * HEAD_DIM)
DK_B = HEAD_DIM
DV_B = HEAD_DIM
CONV_B = 4
FFN_CONV = 3
D_FF = 11008
N_BUCKETS = 32
MAX_DIST = 128
EPS = 1e-5
ALPHA = (2 * DEPTH) ** 0.25
OUT_INIT = (8 * DEPTH) ** -0.25

W_A = N_HEADS_A * HEAD_DIM
W_B = N_HEADS_B * DV_B
KV_W = N_KV_A * HEAD_DIM
IQ_W = IDX_HEADS * IDX_DIM
C_QKV = 2 * N_HEADS_B * DK_B + N_HEADS_B * DV_B
COL_SIZES = (W_A, KV_W, KV_W, IQ_W, IDX_DIM, IDX_HEADS, C_QKV, W_B, N_HEADS_B, N_HEADS_B)
N_IN = sum(COL_SIZES)
IDX_SCALE = (IDX_HEADS ** -0.5) * (IDX_DIM ** -0.5)

kernel_name = "hybrid_dsa_gdn_stream_step"


def split_cols(h, sizes):
    idx = np.cumsum(np.array(sizes))[:-1].tolist()
    return jnp.split(h, idx, axis=-1)


def layer_norm(x, g, b):
    xf = x.astype(jnp.float32)
    mu = jnp.mean(xf, axis=-1, keepdims=True)
    xc = xf - mu
    var = jnp.mean(xc * xc, axis=-1, keepdims=True)
    return (xc * lax.rsqrt(var + EPS) * g + b).astype(x.dtype)


def l2norm(x):
    xf = x.astype(jnp.float32)
    return xf * lax.rsqrt(jnp.sum(xf * xf, axis=-1, keepdims=True) + 1e-6)


def causal_dwconv(x, prev, w, bias):
    width = w.shape[0]
    t = x.shape[1]
    xp = jnp.concatenate([prev.astype(x.dtype), x], axis=1)
    y = bias
    for j in range(width):
        y = y + xp[:, j:j + t] * w[j]
    return y, xp[:, xp.shape[1] - (width - 1):]


def t5_bucket(rel):
    half = N_BUCKETS // 2
    max_exact = half // 2
    side = jnp.where(rel > 0, half, 0)
    n = jnp.abs(rel)
    nf = jnp.maximum(n, 1).astype(jnp.float32)
    large = max_exact + (jnp.log(nf / max_exact) / math.log(MAX_DIST / max_exact)
                         * (half - max_exact)).astype(jnp.int32)
    large = jnp.minimum(large, half - 1)
    return side + jnp.where(n < max_exact, n, large)


def dsa_block(q, iq, iw, q_pos, k_all, v_all, ik_all, top_k, rel_bias):
    b, t, h, hd = q.shape
    n_len = k_all.shape[1]
    k_pos = jnp.arange(n_len, dtype=jnp.int32)
    s_idx = jnp.einsum("bthd,bsd->bths", iq, ik_all).astype(jnp.float32)
    index = jnp.einsum("bths,bth->bts", jax.nn.relu(s_idx), iw.astype(jnp.float32) * IDX_SCALE)
    limit = (q_pos // CHUNK + 1) * CHUNK
    admissible = k_pos[None, :] < limit[:, None]
    index = jnp.where(admissible[None], index, -jnp.inf)
    _, sel = lax.top_k(index, top_k)
    valid = sel < limit[None, :, None]
    gather = jax.vmap(lambda kv, ii: kv[ii])
    k_sel = gather(k_all, sel)
    v_sel = gather(v_all, sel)
    qg = q.reshape(b, t, N_KV_A, h // N_KV_A, hd)
    logits = jnp.einsum("btngd,btsnd->btngs", qg, k_sel).astype(jnp.float32) * (hd ** -0.5)
    bias = rel_bias[t5_bucket(sel - q_pos[None, :, None])].astype(jnp.float32)
    bias = jnp.transpose(bias.reshape(b, t, top_k, N_KV_A, h // N_KV_A), (0, 1, 3, 4, 2))
    logits = jnp.where(valid[:, :, None, None, :], logits + bias, -jnp.inf)
    p = jax.nn.softmax(logits, axis=-1).astype(v_sel.dtype)
    out = jnp.einsum("btngs,btsnd->btngd", p, v_sel)
    return out.reshape(b, t, h * hd)


def gated_delta_chunked(q, k, v, g, beta, s0, chunk):
    b, t, h, dk = q.shape
    dv = v.shape[-1]
    n = t // chunk

    def blk(a):
        return jnp.swapaxes(a.reshape(b, n, chunk, h, *a.shape[3:]), 2, 3)

    q, k, v = blk(q), blk(k), blk(v.astype(jnp.float32))
    g, beta = blk(g), blk(beta)
    G = jnp.cumsum(g, axis=-1)
    tri = jnp.tril(jnp.ones((chunk, chunk), bool))
    strict = jnp.tril(jnp.ones((chunk, chunk), bool), -1)
    dmask = jnp.exp(jnp.where(tri, G[..., :, None] - G[..., None, :], -jnp.inf))
    kb = k * beta[..., None]
    lmat = jnp.where(strict, jnp.einsum("bnhid,bnhjd->bnhij", kb, k) * dmask, 0.0)
    a_mat = lmat + jnp.eye(chunk, dtype=jnp.float32)
    rhs = jnp.concatenate([v * beta[..., None], kb * jnp.exp(G)[..., None]], axis=-1)
    sol = lax.linalg.triangular_solve(a_mat, rhs, left_side=True, lower=True, unit_diagonal=True)
    u, w = sol[..., :dv], sol[..., dv:]
    qk = jnp.where(tri, jnp.einsum("bnhid,bnhjd->bnhij", q, k) * dmask, 0.0)
    q_dec = q * jnp.exp(G)[..., None]
    k_dec = k * jnp.exp(G[..., -1:] - G)[..., None]
    g_last = jnp.exp(G[..., -1])

    def step(S, xs):
        u_c, w_c, qk_c, qd_c, kd_c, gl_c = xs
        v_new = u_c - jnp.einsum("bhcd,bhde->bhce", w_c, S)
        o = jnp.einsum("bhcd,bhde->bhce", qd_c, S) + jnp.einsum("bhij,bhje->bhie", qk_c, v_new)
        S = S * gl_c[..., None, None] + jnp.einsum("bhcd,bhce->bhde", kd_c, v_new)
        return S, o

    xs = tuple(jnp.moveaxis(a, 1, 0) for a in (u, w, qk, q_dec, k_dec, g_last))
    s_fin, o = lax.scan(step, s0.astype(jnp.float32), xs)
    o = jnp.transpose(o, (1, 0, 3, 2, 4)).reshape(b, t, h, dv)
    return o, s_fin


def encoder_layer(x, cache_k, cache_v, cache_ik, s_delta, conv_prev, ffn_prev,
                  w_in, conv_qkv_w, conv_qkv_b, a_log, dt_bias, delta_norm_g, rel_bias, w_o,
                  ln1_g, ln1_b, w_ffn_up, ffn_conv_w, ffn_conv_b, w_ffn_down, ln2_g, ln2_b):
    b, t, _ = x.shape
    past = cache_k.shape[1]
    h = x @ w_in
    q_a, k_a, v_a, iq, ik, iw, qkv_b, z_b, beta_in, a_in = split_cols(h, COL_SIZES)

    q_a = q_a.reshape(b, t, N_HEADS_A, HEAD_DIM)
    k_new = k_a.reshape(b, t, N_KV_A, HEAD_DIM)
    v_new = v_a.reshape(b, t, N_KV_A, HEAD_DIM)
    iq = iq.reshape(b, t, IDX_HEADS, IDX_DIM)
    k_all = jnp.concatenate([cache_k.astype(x.dtype), k_new], axis=1)
    v_all = jnp.concatenate([cache_v.astype(x.dtype), v_new], axis=1)
    ik_all = jnp.concatenate([cache_ik.astype(x.dtype), ik], axis=1)
    top_k = min(TOPK_MAX, (past + t) // 4)
    q_pos = past + jnp.arange(t, dtype=jnp.int32)
    if t > QBLK and t % QBLK == 0:
        nb = t // QBLK

        def to_blocks(a):
            return jnp.moveaxis(a.reshape(b, nb, QBLK, *a.shape[2:]), 1, 0)

        blocks = (to_blocks(q_a), to_blocks(iq), to_blocks(iw), q_pos.reshape(nb, QBLK))
        out_a = lax.map(lambda args: dsa_block(args[0], args[1], args[2], args[3],
                                               k_all, v_all, ik_all, top_k, rel_bias), blocks)
        out_a = jnp.moveaxis(out_a, 0, 1).reshape(b, t, W_A)
    else:
        out_a = dsa_block(q_a, iq, iw, q_pos, k_all, v_all, ik_all, top_k, rel_bias)

    conv_out, conv_state = causal_dwconv(qkv_b, conv_prev, conv_qkv_w, conv_qkv_b)
    conv_out = jax.nn.silu(conv_out)
    q_b, k_b, v_b = split_cols(conv_out, (N_HEADS_B * DK_B, N_HEADS_B * DK_B, N_HEADS_B * DV_B))
    q_b = l2norm(q_b.reshape(b, t, N_HEADS_B, DK_B)) * (DK_B ** -0.5)
    k_b = l2norm(k_b.reshape(b, t, N_HEADS_B, DK_B))
    v_b = v_b.reshape(b, t, N_HEADS_B, DV_B)
    beta = jax.nn.sigmoid(beta_in.astype(jnp.float32))
    g = -jnp.exp(a_log.astype(jnp.float32)) * jax.nn.softplus(a_in.astype(jnp.float32) + dt_bias)
    chunk = CHUNK if t % CHUNK == 0 else t
    o_b, s_new = gated_delta_chunked(q_b, k_b, v_b, g, beta, s_delta, chunk)
    o_b = o_b * lax.rsqrt(jnp.mean(o_b * o_b, axis=-1, keepdims=True) + EPS) * delta_norm_g
    o_b = o_b * jax.nn.silu(z_b.astype(jnp.float32).reshape(b, t, N_HEADS_B, DV_B))
    o_b = o_b.reshape(b, t, W_B).astype(x.dtype)

    mix = jnp.concatenate([out_a, o_b], axis=-1) @ w_o
    x1 = layer_norm(ALPHA * x + mix, ln1_g, ln1_b)

    up = x1 @ w_ffn_up
    up_c, ffn_state = causal_dwconv(up, ffn_prev, ffn_conv_w, ffn_conv_b)
    gate, val = jnp.split(up_c, 2, axis=-1)
    ffn = (jax.nn.silu(gate) * val) @ w_ffn_down
    y = layer_norm(ALPHA * x1 + ffn, ln2_g, ln2_b)
    return y, (k_new, v_new, ik, s_new.astype(s_delta.dtype), conv_state, ffn_state)


def setup_inputs(seed: int = 0) -> dict:
    key = jax.random.key(seed)
    ks = jax.random.split(key, 26)
    f32 = jnp.float32

    def nrm(k, shape, scale):
        return jax.random.normal(k, shape, f32) * scale

    dt = jnp.exp(jax.random.uniform(ks[13], (DEPTH, N_HEADS_B), f32,
                                    math.log(1e-3), math.log(1e-1)))
    return {
        "x_prompt": nrm(ks[0], (BATCH, SEQ, D_MODEL), 1.0),
        "x_sample": nrm(ks[1], (DEC_BATCH, DEC_SEQ, D_MODEL), 1.0),
        "cache_attn_k": nrm(ks[2], (DEPTH, DEC_BATCH, PAST_LEN, N_KV_A, HEAD_DIM), 1.0),
        "cache_attn_v": nrm(ks[3], (DEPTH, DEC_BATCH, PAST_LEN, N_KV_A, HEAD_DIM), 1.0),
        "cache_idx_k": nrm(ks[4], (DEPTH, DEC_BATCH, PAST_LEN, IDX_DIM), 1.0),
        "state_delta": nrm(ks[5], (DEPTH, DEC_BATCH, N_HEADS_B, DK_B, DV_B), 0.1),
        "state_conv_qkv": nrm(ks[6], (DEPTH, DEC_BATCH, CONV_B - 1, C_QKV), 1.0),
        "state_ffn_conv": nrm(ks[7], (DEPTH, DEC_BATCH, FFN_CONV - 1, 2 * D_FF), 1.0),
        "ln_in_g": 1.0 + nrm(ks[8], (D_MODEL,), 0.01),
        "ln_in_b": nrm(ks[9], (D_MODEL,), 0.01),
        "w_in": nrm(ks[10], (DEPTH, D_MODEL, N_IN), D_MODEL ** -0.5),
        "conv_qkv_w": nrm(ks[11], (DEPTH, CONV_B, C_QKV), CONV_B ** -0.5),
        "conv_qkv_b": nrm(ks[12], (DEPTH, C_QKV), 0.01),
        "a_log": jnp.log(jax.random.uniform(ks[14], (DEPTH, N_HEADS_B), f32, 1.0, 16.0)),
        "dt_bias": dt + jnp.log(-jnp.expm1(-dt)),
        "delta_norm_g": 1.0 + nrm(ks[15], (DEPTH, DV_B), 0.01),
        "rel_bias": nrm(ks[16], (N_BUCKETS, N_HEADS_A), 0.1),
        "w_o": nrm(ks[17], (DEPTH, W_A + W_B, D_MODEL), (W_A + W_B) ** -0.5 * OUT_INIT),
        "ln1_g": 1.0 + nrm(ks[18], (DEPTH, D_MODEL), 0.01),
        "ln1_b": nrm(ks[19], (DEPTH, D_MODEL), 0.01),
        "w_ffn_up": nrm(ks[20], (DEPTH, D_MODEL, 2 * D_FF), D_MODEL ** -0.5),
        "ffn_conv_w": nrm(ks[21], (DEPTH, FFN_CONV, 2 * D_FF), FFN_CONV ** -0.5),
        "ffn_conv_b": nrm(ks[22], (DEPTH, 2 * D_FF), 0.01),
        "w_ffn_down": nrm(ks[23], (DEPTH, D_FF, D_MODEL), D_FF ** -0.5 * OUT_INIT),
        "ln2_g": 1.0 + nrm(ks[24], (DEPTH, D_MODEL), 0.01),
        "ln2_b": nrm(ks[25], (DEPTH, D_MODEL), 0.01),
    }


def reference(x_prompt, x_sample, cache_attn_k, cache_attn_v, cache_idx_k, state_delta,
              state_conv_qkv, state_ffn_conv, ln_in_g, ln_in_b, w_in, conv_qkv_w, conv_qkv_b,
              a_log, dt_bias, delta_norm_g, rel_bias, w_o, ln1_g, ln1_b, w_ffn_up, ffn_conv_w,
              ffn_conv_b, w_ffn_down, ln2_g, ln2_b):
    xp = layer_norm(x_prompt, ln_in_g, ln_in_b)
    xs = layer_norm(x_sample, ln_in_g, ln_in_b)
    b = x_prompt.shape[0]
    dt_ = x_prompt.dtype
    p_states = []
    s_states = []
    for l in range(DEPTH):
        weights = (w_in[l], conv_qkv_w[l], conv_qkv_b[l], a_log[l], dt_bias[l], delta_norm_g[l],
                   rel_bias, w_o[l], ln1_g[l], ln1_b[l], w_ffn_up[l], ffn_conv_w[l],
                   ffn_conv_b[l], w_ffn_down[l], ln2_g[l], ln2_b[l])
        xp, st_p = encoder_layer(
            xp,
            jnp.zeros((b, 0, N_KV_A, HEAD_DIM), dt_),
            jnp.zeros((b, 0, N_KV_A, HEAD_DIM), dt_),
            jnp.zeros((b, 0, IDX_DIM), dt_),
            jnp.zeros((b, N_HEADS_B, DK_B, DV_B), jnp.float32),
            jnp.zeros((b, CONV_B - 1, C_QKV), dt_),
            jnp.zeros((b, FFN_CONV - 1, 2 * D_FF), dt_),
            *weights)
        xs, st_s = encoder_layer(xs, cache_attn_k[l], cache_attn_v[l], cache_idx_k[l],
                                 state_delta[l], state_conv_qkv[l], state_ffn_conv[l], *weights)
        p_states.append(st_p)
        s_states.append(st_s)
    p_k, p_v, p_ik, p_delta, p_conv, p_ffn = [jnp.stack(z) for z in zip(*p_states)]
    s_k, s_v, s_ik, s_delta, s_conv, s_ffn = [jnp.stack(z) for z in zip(*s_states)]
    return (xp, xs, p_k, p_v, p_ik, p_delta, p_conv, p_ffn,
            s_k, s_v, s_ik, s_delta, s_conv, s_ffn)
```

```python
import functools
import math

import numpy as np
import jax
import jax.numpy as jnp
from jax import lax
from jax.experimental import pallas as pl
from jax.experimental.pallas import tpu as pltpu

F32 = jnp.float32
BF16 = jnp.bfloat16

CHUNK = 64
HEAD_DIM = 128
N_KV_A = 4
IDX_HEADS = 16
IDX_DIM = 64
TOPK_MAX = 256
CONV_B = 4
FFN_CONV = 3
N_BUCKETS = 32
MAX_DIST = 128
EPS = 1e-5
IDX_SCALE = (IDX_HEADS ** -0.5) * (IDX_DIM ** -0.5)

LANES = 128
SUBLANES = 8
BF16_ROWS = 16
VMEM_LIMIT = 56 * 1024 * 1024

KEY_BLOCK = LANES
NEG = -1e30
INT_MIN = -2 ** 31

NT_DIMS = (((1,), (1,)), ((), ()))
TN_DIMS = (((0,), (0,)), ((), ()))

SM_IK = 0
SM_IW = IDX_DIM
SM_BETA = SM_IW + IDX_HEADS
SM_A = SM_BETA + 16


def _params(n_grid_axes):
    return pltpu.CompilerParams(dimension_semantics=("arbitrary",) * n_grid_axes,
                                vmem_limit_bytes=VMEM_LIMIT)


def _dot(a, b, dims=None):
    a = a.astype(BF16)
    b = b.astype(BF16)
    if dims is None:
        return jnp.dot(a, b, preferred_element_type=F32)
    return lax.dot_general(a, b, dims, preferred_element_type=F32)


def _dot_exact(a, b):
    return jnp.dot(a, b, preferred_element_type=F32, precision=lax.Precision.HIGHEST)


def _dot3(a, b):
    ah = a.astype(BF16)
    al = (a - ah.astype(F32)).astype(BF16)
    bh = b.astype(BF16)
    bl = (b - bh.astype(F32)).astype(BF16)
    d = functools.partial(jnp.dot, preferred_element_type=F32)
    return d(ah, bh) + (d(ah, bl) + d(al, bh))


def _silu(x):
    return x * (1.0 / (1.0 + jnp.exp(-x)))


def _ln_kernel(x_ref, g_ref, b_ref, *out_refs):
    x = x_ref[...]
    mu = jnp.mean(x, axis=-1, keepdims=True)
    xc = x - mu
    var = jnp.mean(xc * xc, axis=-1, keepdims=True)
    y = xc * lax.rsqrt(var + EPS) * g_ref[...] + b_ref[...]
    for o_ref in out_refs:
        o_ref[...] = y.astype(o_ref.dtype)


def _layer_norm(x, g, b, out_dtypes):
    m, d = x.shape
    tr = _pick(m, (256, 128, 64))
    row = pl.BlockSpec((tr, d), lambda i: (i, 0))
    vec = pl.BlockSpec((1, d), lambda i: (0, 0))
    outs = pl.pallas_call(
        _ln_kernel,
        out_shape=[jax.ShapeDtypeStruct((m, d), dt) for dt in out_dtypes],
        grid=(m // tr,),
        in_specs=[row, vec, vec],
        out_specs=[row for _ in out_dtypes],
        compiler_params=_params(1),
        name="layer_norm",
    )(x, g.reshape(1, d), b.reshape(1, d))
    return outs


def _mm_kernel(*refs, k_sizes, alpha, has_resid):
    n_lhs = len(k_sizes)
    w_ref = refs[n_lhs]
    o_ref = refs[-1]
    acc = None
    off = 0
    for a_ref, kp in zip(refs[:n_lhs], k_sizes):
        d = jnp.dot(a_ref[...], w_ref[off:off + kp, :], preferred_element_type=F32)
        acc = d if acc is None else acc + d
        off += kp
    if has_resid:
        acc = acc + alpha * refs[n_lhs + 1][...]
    o_ref[...] = acc.astype(o_ref.dtype)


def _matmul(lhs_list, w, *, tm, tn, out_dtype, resid=None, alpha=1.0):
    m = lhs_list[0].shape[0]
    k_sizes = tuple(a.shape[1] for a in lhs_list)
    k, n = w.shape
    assert sum(k_sizes) == k and m % tm == 0 and n % tn == 0
    in_specs = [pl.BlockSpec((tm, kp), lambda i, j: (i, 0)) for kp in k_sizes]
    in_specs.append(pl.BlockSpec((k, tn), lambda i, j: (0, j)))
    args = list(lhs_list) + [w]
    if resid is not None:
        in_specs.append(pl.BlockSpec((tm, tn), lambda i, j: (i, j)))
        args.append(resid)
    return pl.pallas_call(
        functools.partial(_mm_kernel, k_sizes=k_sizes, alpha=alpha, has_resid=resid is not None),
        out_shape=jax.ShapeDtypeStruct((m, n), out_dtype),
        grid=(m // tm, n // tn),
        in_specs=in_specs,
        out_specs=pl.BlockSpec((tm, tn), lambda i, j: (i, j)),
        compiler_params=_params(2),
        name="matmul",
    )(*args)


def _t5_bucket_np(rel):
    half = N_BUCKETS // 2
    max_exact = half // 2
    n = np.abs(rel)
    large = max_exact + np.floor(2.0 * np.log2(np.maximum(n, 1) / max_exact) + 1e-9).astype(np.int64)
    large = np.minimum(large, half - 1)
    return np.where(rel > 0, half, 0) + np.where(n < max_exact, n, large)


def _bias_tiles(rel_bias, tq):
    n_heads = rel_bias.shape[1]
    group = n_heads // N_KV_A
    i = np.arange(tq)[:, None]
    j = np.arange(KEY_BLOCK)[None, :]
    tiles = []
    for r in range(3):
        rel = -r * KEY_BLOCK + j - i
        bucket = _t5_bucket_np(rel)
        if r == 2:
            assert (bucket == N_BUCKETS // 2 - 1).all()
        t = rel_bias[bucket.reshape(-1)].reshape(tq, KEY_BLOCK, N_KV_A, group)
        tiles.append(jnp.transpose(t, (2, 3, 0, 1)).reshape(N_KV_A, group * tq, KEY_BLOCK))
    return jnp.stack(tiles).astype(F32)


def _sortable(x):
    i = lax.bitcast_convert_type(x, jnp.int32)
    return jnp.where(i >= 0, i, i ^ jnp.int32(0x7FFFFFFF))


def _dsa_kernel(q_ref, iq_ref, sm_ref, k_ref, v_ref, ik_ref, bias_ref, o_ref,
                key_scr, mask_scr, logit_scr, m_scr, l_scr, acc_scr, *, tq, qb0, top_k, group):
    qb = pl.program_id(1) + qb0
    nkb = qb + 1
    rows = group * tq

    row = lax.broadcasted_iota(jnp.int32, (tq, KEY_BLOCK), 0)
    lane = lax.broadcasted_iota(jnp.int32, (tq, KEY_BLOCK), 1)
    limit = ((qb * KEY_BLOCK + row) // CHUNK + 1) * CHUNK
    lane_k = lax.broadcasted_iota(jnp.int32, (KEY_BLOCK, LANES), 1)

    iw = sm_ref[:, SM_IW:SM_IW + IDX_HEADS] * IDX_SCALE
    iqb = iq_ref[...].astype(BF16)

    def idx_body(r, carry):
        kb = qb - r
        ik = ik_ref[pl.ds(pl.multiple_of(kb * KEY_BLOCK, KEY_BLOCK), KEY_BLOCK), :]
        ik_lo = jnp.where(lane_k < IDX_DIM, ik, 0.0)
        ik_hi = pltpu.roll(ik_lo, IDX_DIM, axis=1)
        rhs = jnp.concatenate([ik_lo, ik_hi], axis=0).astype(BF16)
        acc = jnp.zeros((tq, KEY_BLOCK), F32)
        for g in range(IDX_HEADS // 2):
            s = lax.dot_general(iqb[:, g * LANES:(g + 1) * LANES], rhs, NT_DIMS,
                                preferred_element_type=F32)
            acc = acc + jnp.maximum(s[:, :KEY_BLOCK], 0.0) * iw[:, 2 * g:2 * g + 1]
            acc = acc + jnp.maximum(s[:, KEY_BLOCK:], 0.0) * iw[:, 2 * g + 1:2 * g + 2]
        admissible = kb * KEY_BLOCK + lane < limit
        key_scr[r] = jnp.where(admissible, _sortable(acc), INT_MIN)
        return carry

    lax.fori_loop(0, nkb, idx_body, 0)

    def count_ge(cand):
        cand_b = jnp.broadcast_to(cand, (tq, KEY_BLOCK))

        def body(r, c):
            return c + jnp.where(key_scr[r] >= cand_b, 1.0, 0.0)

        c = lax.fori_loop(0, nkb, body, jnp.zeros((tq, KEY_BLOCK), F32))
        return jnp.sum(c, axis=1, keepdims=True)

    kf = float(top_k)
    zero = jnp.zeros((tq, 1), jnp.int32)
    thr = jnp.where(count_ge(zero) >= kf, zero, jnp.int32(INT_MIN))

    def bit_body(it, thr):
        cand = thr | jnp.left_shift(jnp.int32(1), 30 - it)
        return jnp.where(count_ge(cand) >= kf, cand, thr)

    thr = lax.fori_loop(0, 31, bit_body, thr)
    thr_b = jnp.broadcast_to(thr, (tq, KEY_BLOCK))

    def mask_body(r, carry):
        key = key_scr[r]
        sel = (key >= thr_b) & (key != INT_MIN)
        m = jnp.where(sel, 0.0, NEG)
        mask_scr[r] = jnp.concatenate([m] * group, axis=0)
        return carry

    lax.fori_loop(0, nkb, mask_body, 0)

    scale = HEAD_DIM ** -0.5
    for n in range(N_KV_A):
        q4 = jnp.concatenate(
            [q_ref[:, (group * n + g) * HEAD_DIM:(group * n + g + 1) * HEAD_DIM] for g in range(group)],
            axis=0).astype(BF16)
        m_scr[...] = jnp.full((rows, KEY_BLOCK), NEG, F32)

        def logits_body(r, carry):
            kb = qb - r
            kblk = k_ref[pl.ds(pl.multiple_of(kb * KEY_BLOCK, KEY_BLOCK), KEY_BLOCK),
                         n * HEAD_DIM:(n + 1) * HEAD_DIM]
            s = lax.dot_general(q4, kblk.astype(BF16), NT_DIMS, preferred_element_type=F32) * scale
            s = s + bias_ref[jnp.minimum(r, 2), n] + mask_scr[r]
            logit_scr[r] = s
            m_scr[...] = jnp.maximum(m_scr[...], s)
            return carry

        lax.fori_loop(0, nkb, logits_body, 0)
        m = jnp.max(m_scr[...], axis=1, keepdims=True)
        l_scr[...] = jnp.zeros((rows, KEY_BLOCK), F32)
        acc_scr[...] = jnp.zeros((rows, HEAD_DIM), F32)

        def pv_body(r, carry):
            kb = qb - r
            p = jnp.exp(logit_scr[r] - m)
            vblk = v_ref[pl.ds(pl.multiple_of(kb * KEY_BLOCK, KEY_BLOCK), KEY_BLOCK),
                         n * HEAD_DIM:(n + 1) * HEAD_DIM]
            l_scr[...] += p
            acc_scr[...] += jnp.dot(p.astype(BF16), vblk.astype(BF16), preferred_element_type=F32)
            return carry

        lax.fori_loop(0, nkb, pv_body, 0)
        out = acc_scr[...] / jnp.sum(l_scr[...], axis=1, keepdims=True)
        for g in range(group):
            h = group * n + g
            o_ref[:, h * HEAD_DIM:(h + 1) * HEAD_DIM] = out[g * tq:(g + 1) * tq].astype(o_ref.dtype)


def _dsa(q_src, q_blk, iq_src, iq_blk, small, k_src, k_blk, v_src, v_blk, ik_src, bias, *,
         t, tq, qb0, top_k, n_heads):
    b = q_src.shape[0]
    l = k_src.shape[1]
    group = n_heads // N_KV_A
    nq = t // tq
    nkb_max = qb0 + nq
    assert l == nkb_max * KEY_BLOCK and t % tq == 0 and tq % CHUNK == 0 and tq <= KEY_BLOCK
    assert tq == KEY_BLOCK or nq == 1
    w_a = n_heads * HEAD_DIM
    w_kv = N_KV_A * HEAD_DIM
    w_iq = IDX_HEADS * IDX_DIM
    rows = group * tq
    kern = functools.partial(_dsa_kernel, tq=tq, qb0=qb0, top_k=top_k, group=group)
    return pl.pallas_call(
        kern,
        out_shape=jax.ShapeDtypeStruct((b, t, w_a), BF16),
        grid=(b, nq),
        in_specs=[
            pl.BlockSpec((None, tq, w_a), lambda bi, i: (bi, i, q_blk)),
            pl.BlockSpec((None, tq, w_iq), lambda bi, i: (bi, i, iq_blk)),
            pl.BlockSpec((None, tq, LANES), lambda bi, i: (bi, i, 0)),
            pl.BlockSpec((None, l, w_kv), lambda bi, i: (bi, 0, k_blk)),
            pl.BlockSpec((None, l, w_kv), lambda bi, i: (bi, 0, v_blk)),
            pl.BlockSpec((None, l, LANES), lambda bi, i: (bi, 0, 0)),
            pl.BlockSpec((3, N_KV_A, rows, KEY_BLOCK), lambda bi, i: (0, 0, 0, 0)),
        ],
        out_specs=pl.BlockSpec((None, tq, w_a), lambda bi, i: (bi, i, 0)),
        scratch_shapes=[
            pltpu.VMEM((nkb_max, tq, KEY_BLOCK), jnp.int32),
            pltpu.VMEM((nkb_max, rows, KEY_BLOCK), F32),
            pltpu.VMEM((nkb_max, rows, KEY_BLOCK), F32),
            pltpu.VMEM((rows, KEY_BLOCK), F32),
            pltpu.VMEM((rows, KEY_BLOCK), F32),
            pltpu.VMEM((rows, HEAD_DIM), F32),
        ],
        compiler_params=_params(2),
        name="dsa",
    )(q_src, iq_src, small, k_src, v_src, ik_src, bias)


def _gdn_kernel(xq_ref, xk_ref, xv_ref, z_ref, sm_ref, pq_ref, pk_ref, pv_ref,
                wq_ref, wk_ref, wv_ref, bq_ref, bk_ref, bv_ref, alog_ref, dtb_ref, ng_ref, s0_ref,
                o_ref, s_ref, gc_scr, gt_scr, *, t, hb):
    hg = pl.program_id(1)
    n_chunks = t // CHUNK
    blk = min(t, LANES)
    n_blk = t // blk

    ri = lax.broadcasted_iota(jnp.int32, (blk, blk), 0)
    ci = lax.broadcasted_iota(jnp.int32, (blk, blk), 1)
    tril_blocks = jnp.where((ri >= ci) & (ri // CHUNK == ci // CHUNK), 1.0, 0.0)
    a_scale = -jnp.exp(alog_ref[...])
    dtb = dtb_ref[...]
    for bi in range(n_blk):
        a = sm_ref[bi * blk:(bi + 1) * blk, :] + dtb
        g = a_scale * (jnp.maximum(a, 0.0) + jnp.log1p(jnp.exp(-jnp.abs(a))))
        gc = _dot_exact(tril_blocks, g)
        gc_scr[bi * blk:(bi + 1) * blk, :] = gc
        if blk < LANES:
            gc = jnp.concatenate([gc, jnp.zeros((LANES - blk, LANES), F32)], axis=0)
        gt = gc.T
        for half in range(blk // CHUNK):
            gt_scr[bi * (blk // CHUNK) + half] = gt[:, half * CHUNK:(half + 1) * CHUNK]

    lane = lax.broadcasted_iota(jnp.int32, (CHUNK, LANES), 1)
    ri = lax.broadcasted_iota(jnp.int32, (CHUNK, CHUNK), 0)
    ci = lax.broadcasted_iota(jnp.int32, (CHUNK, CHUNK), 1)
    tri = ri >= ci
    strict = ri > ci
    eye = jnp.where(ri == ci, 1.0, 0.0)
    ng = ng_ref[...]

    def conv(win, w_ref, b_ref, hh):
        cols = slice(hh * HEAD_DIM, (hh + 1) * HEAD_DIM)
        y = b_ref[:, cols]
        for j in range(CONV_B):
            lo = SUBLANES - (CONV_B - 1) + j
            y = y + win[lo:lo + CHUNK, cols] * w_ref[j:j + 1, cols]
        return _silu(y)

    def l2n(x):
        return x * lax.rsqrt(jnp.sum(x * x, axis=-1, keepdims=True) + 1e-6)

    def chunk_step(c, row0, wins, states):
        wq, wk, wv = wins
        sm = sm_ref[pl.ds(row0, CHUNK), :]
        beta_all = 1.0 / (1.0 + jnp.exp(-sm))
        gc_all = gc_scr[pl.ds(row0, CHUNK), :]
        new_states = []
        for hh in range(hb):
            h = hg * hb + hh
            cols = slice(hh * HEAD_DIM, (hh + 1) * HEAD_DIM)
            beta = jnp.sum(jnp.where(lane == SM_BETA + h, beta_all, 0.0), axis=1, keepdims=True)
            gcol = jnp.sum(jnp.where(lane == SM_A + h, gc_all, 0.0), axis=1, keepdims=True)
            grow = gt_scr[c, pl.ds(SM_A + h, 1), :]
            glast = gcol[CHUNK - 1:CHUNK, :]
            diff = gcol - grow
            dmask = jnp.where(tri, jnp.exp(jnp.where(tri, diff, 0.0)), 0.0)

            q = l2n(conv(wq, wq_ref, bq_ref, hh)) * (HEAD_DIM ** -0.5)
            k = l2n(conv(wk, wk_ref, bk_ref, hh))
            v = conv(wv, wv_ref, bv_ref, hh)
            kb = k * beta
            lmat = jnp.where(strict, _dot(kb, k, NT_DIMS) * dmask, 0.0)
            npow = -lmat
            ainv = eye + npow
            for _ in range(int(math.log2(CHUNK)) - 1):
                npow = _dot3(npow, npow)
                ainv = ainv + _dot3(ainv, npow)
            eg = jnp.exp(gcol)
            rhs = jnp.concatenate([v * beta, kb * eg], axis=1)
            sol = _dot3(ainv, rhs)
            u, w = sol[:, :HEAD_DIM], sol[:, HEAD_DIM:]
            qk = jnp.where(tri, _dot(q, k, NT_DIMS) * dmask, 0.0)
            q_dec = q * eg
            k_dec = k * jnp.exp(glast - gcol)

            s = states[hh]
            v_new = u - _dot(w, s)
            o = _dot(q_dec, s) + _dot(qk, v_new)
            s = s * jnp.exp(glast) + _dot(k_dec, v_new, TN_DIMS)
            new_states.append(s)

            o = o * lax.rsqrt(jnp.mean(o * o, axis=-1, keepdims=True) + EPS) * ng
            o = o * _silu(z_ref[pl.ds(row0, CHUNK), cols])
            o_ref[pl.ds(row0, CHUNK), cols] = o.astype(o_ref.dtype)
        return tuple(new_states)

    states = tuple(s0_ref[hh] for hh in range(hb))
    first = tuple(jnp.concatenate([p_ref[...], x_ref[0:CHUNK, :]], axis=0)
                  for p_ref, x_ref in ((pq_ref, xq_ref), (pk_ref, xk_ref), (pv_ref, xv_ref)))
    states = chunk_step(0, 0, first, states)

    def body(c, states):
        row0 = pl.multiple_of(c * CHUNK, CHUNK)
        wins = tuple(x_ref[pl.ds(row0 - SUBLANES, CHUNK + SUBLANES), :] for x_ref in (xq_ref, xk_ref, xv_ref))
        return chunk_step(c, row0, wins, states)

    if n_chunks > 1:
        states = lax.fori_loop(1, n_chunks, body, states)
    for hh in range(hb):
        s_ref[hh] = states[hh]


def _gdn(h_main, qkv_col0, z_col0, small, prev8, conv_w, conv_b, alog_row, dtb_row, norm_g, s0, *, hb):
    b, t, _ = h_main.shape
    n_heads = s0.shape[1]
    w_b = n_heads * HEAD_DIM
    hw = hb * HEAD_DIM
    ng = n_heads // hb
    assert t % CHUNK == 0 and (t % LANES == 0 or t == CHUNK) and n_heads % hb == 0

    def col(c0):
        return lambda bi, g: (bi, 0, c0 // hw + g)

    def wcol(c0):
        return lambda bi, g: (0, c0 // hw + g)

    tok = lambda c0: pl.BlockSpec((None, t, hw), col(c0))
    prev = lambda c0: pl.BlockSpec((None, SUBLANES, hw), col(c0))
    cw = lambda c0: pl.BlockSpec((CONV_B, hw), wcol(c0))
    cb = lambda c0: pl.BlockSpec((1, hw), wcol(c0))
    vec = pl.BlockSpec((1, LANES), lambda bi, g: (0, 0))
    state = pl.BlockSpec((None, hb, HEAD_DIM, HEAD_DIM), lambda bi, g: (bi, g, 0, 0))
    return pl.pallas_call(
        functools.partial(_gdn_kernel, t=t, hb=hb),
        out_shape=[jax.ShapeDtypeStruct((b, t, w_b), BF16),
                   jax.ShapeDtypeStruct(s0.shape, F32)],
        grid=(b, ng),
        in_specs=[tok(qkv_col0), tok(qkv_col0 + w_b), tok(qkv_col0 + 2 * w_b), tok(z_col0),
                  pl.BlockSpec((None, t, LANES), lambda bi, g: (bi, 0, 0)),
                  prev(0), prev(w_b), prev(2 * w_b),
                  cw(0), cw(w_b), cw(2 * w_b), cb(0), cb(w_b), cb(2 * w_b),
                  vec, vec, vec, state],
        out_specs=[pl.BlockSpec((None, t, hw), lambda bi, g: (bi, 0, g)), state],
        scratch_shapes=[pltpu.VMEM((t, LANES), F32),
                        pltpu.VMEM((t // CHUNK, LANES, CHUNK), F32)],
        compiler_params=_params(2),
        name="gdn",
    )(h_main, h_main, h_main, h_main, small, prev8, prev8, prev8,
      conv_w, conv_w, conv_w, conv_b, conv_b, conv_b, alog_row, dtb_row, norm_g, s0)


def _ffn_up_kernel(*refs, tm, seg, has_prev):
    x_ref, halo_ref, wg_ref, wv_ref, cwg_ref, cwv_ref, cbg_ref, cbv_ref = refs[:8]
    o_ref = refs[-1]
    i = pl.program_id(0)
    tn = o_ref.shape[1]
    xe = jnp.concatenate([halo_ref[...], x_ref[...]], axis=0)
    r = (i * tm + lax.broadcasted_iota(jnp.int32, (tm, tn), 0)) % seg

    if has_prev:
        nseg = tm // seg
        srow = lax.broadcasted_iota(jnp.int32, (tm, nseg), 0)
        scol = lax.broadcasted_iota(jnp.int32, (tm, nseg), 1) * seg
        place0 = jnp.where(srow == scol, 1.0, 0.0)
        place1 = jnp.where(srow == scol + 1, 1.0, 0.0)

    def branch(w_ref, cw_ref, cb_ref, prev_refs):
        up = jnp.dot(xe, w_ref[...], preferred_element_type=F32)
        y = cb_ref[...] + up[BF16_ROWS:, :] * cw_ref[2:3, :]
        y = y + jnp.where(r >= 1, up[BF16_ROWS - 1:BF16_ROWS - 1 + tm, :], 0.0) * cw_ref[1:2, :]
        y = y + jnp.where(r >= 2, up[BF16_ROWS - 2:BF16_ROWS - 2 + tm, :], 0.0) * cw_ref[0:1, :]
        if has_prev:
            p0, p1 = prev_refs[0][...], prev_refs[1][...]
            y = y + _dot_exact(place0, p0 * cw_ref[0:1, :] + p1 * cw_ref[1:2, :])
            y = y + _dot_exact(place1, p1 * cw_ref[0:1, :])
        return y

    gate = branch(wg_ref, cwg_ref, cbg_ref, refs[8:10])
    val = branch(wv_ref, cwv_ref, cbv_ref, refs[10:12])
    o_ref[...] = (_silu(gate) * val).astype(o_ref.dtype)


def _ffn_up(x1b, w_up, conv_w, conv_b, prev, *, seg, tm, tn):
    m, d = x1b.shape
    f = w_up.shape[1] // 2
    nj = f // tn
    has_prev = prev is not None
    assert m % tm == 0 and f % tn == 0 and (seg % tm == 0 or tm % seg == 0) and tm % BF16_ROWS == 0
    hrows = tm // BF16_ROWS
    in_specs = [
        pl.BlockSpec((tm, d), lambda i, j: (i, 0)),
        pl.BlockSpec((BF16_ROWS, d), lambda i, j: (jnp.maximum(i * hrows - 1, 0), 0)),
        pl.BlockSpec((d, tn), lambda i, j: (0, j)),
        pl.BlockSpec((d, tn), lambda i, j: (0, nj + j)),
        pl.BlockSpec((FFN_CONV, tn), lambda i, j: (0, j)),
        pl.BlockSpec((FFN_CONV, tn), lambda i, j: (0, nj + j)),
        pl.BlockSpec((1, tn), lambda i, j: (0, j)),
        pl.BlockSpec((1, tn), lambda i, j: (0, nj + j)),
    ]
    args = [x1b, x1b, w_up, w_up, conv_w, conv_w, conv_b, conv_b]
    if has_prev:
        nseg = tm // seg
        p0, p1 = prev[:, 0, :], prev[:, 1, :]
        in_specs += [pl.BlockSpec((nseg, tn), lambda i, j: (i, j)),
                     pl.BlockSpec((nseg, tn), lambda i, j: (i, j)),
                     pl.BlockSpec((nseg, tn), lambda i, j: (i, nj + j)),
                     pl.BlockSpec((nseg, tn), lambda i, j: (i, nj + j))]
        args += [p0, p1, p0, p1]
    return pl.pallas_call(
        functools.partial(_ffn_up_kernel, tm=tm, seg=seg, has_prev=has_prev),
        out_shape=jax.ShapeDtypeStruct((m, f), BF16),
        grid=(m // tm, nj),
        in_specs=in_specs,
        out_specs=pl.BlockSpec((tm, tn), lambda i, j: (i, j)),
        compiler_params=_params(2),
        name="ffn_up",
    )(*args)


def _pick(n, candidates):
    for c in candidates:
        if n % c == 0:
            return c
    raise ValueError(f"no tile for {n}")


def _layer(xn, xnb, cache, w, *, b, t):
    m, d = xn.shape
    n_heads = w["n_heads"]
    w_a = n_heads * HEAD_DIM
    w_kv = N_KV_A * HEAD_DIM
    w_iq = IDX_HEADS * IDX_DIM
    c_qkv = 3 * w_a
    tm = _pick(m, (1024, 512, 256, 128, 64))

    h_main = _matmul([xnb], w["w_main"], tm=tm, tn=512, out_dtype=F32)
    small = _matmul([xnb], w["w_small"], tm=tm, tn=LANES, out_dtype=F32)
    h3 = h_main.reshape(b, t, -1)
    small3 = small.reshape(b, t, LANES)
    col_k, col_v, col_iq, col_qkv, col_z = w_a, w_a + w_kv, w_a + 2 * w_kv, w_a + 2 * w_kv + w_iq, None
    col_z = col_qkv + c_qkv
    k_new = h3[:, :, col_k:col_k + w_kv]
    v_new = h3[:, :, col_v:col_v + w_kv]
    ik_new = small3[:, :, SM_IK:SM_IK + IDX_DIM]
    qkv_b = h3[:, :, col_qkv:col_qkv + c_qkv]

    if cache is None:
        past = 0
        tq = KEY_BLOCK
        assert t % KEY_BLOCK == 0
        k_src, k_blk, v_src, v_blk, ik_src = h3, col_k // w_kv, h3, col_v // w_kv, small3
    else:
        past = cache["k"].shape[1]
        tq = t
        assert past % KEY_BLOCK == 0 and t <= KEY_BLOCK
        pad = KEY_BLOCK - t
        k_src = jnp.concatenate([cache["k"].reshape(b, past, w_kv), k_new,
                                 jnp.zeros((b, pad, w_kv), F32)], axis=1)
        v_src = jnp.concatenate([cache["v"].reshape(b, past, w_kv), v_new,
                                 jnp.zeros((b, pad, w_kv), F32)], axis=1)
        ik_all = jnp.concatenate([cache["ik"], ik_new, jnp.zeros((b, pad, IDX_DIM), F32)], axis=1)
        ik_src = jnp.pad(ik_all, ((0, 0), (0, 0), (0, LANES - IDX_DIM)))
        k_blk = v_blk = 0
    top_k = min(TOPK_MAX, (past + t) // 4)
    out_a = _dsa(h3, 0, h3, col_iq // w_iq, small3, k_src, k_blk, v_src, v_blk, ik_src,
                 _bias_tiles(w["rel_bias"], tq), t=t, tq=tq, qb0=past // KEY_BLOCK, top_k=top_k,
                 n_heads=n_heads)

    if cache is None:
        prev8 = jnp.zeros((b, SUBLANES, c_qkv), F32)
        s0 = jnp.zeros((b, n_heads, HEAD_DIM, HEAD_DIM), F32)
    else:
        prev8 = jnp.pad(cache["conv"], ((0, 0), (SUBLANES - (CONV_B - 1), 0), (0, 0)))
        s0 = cache["delta"]
    o_b, s_new = _gdn(h3, col_qkv, col_z, small3, prev8, w["conv_qkv_w"], w["conv_qkv_b"],
                      w["alog_row"], w["dtb_row"], w["norm_g"], s0, hb=4)
    if t >= CONV_B - 1:
        conv_state = qkv_b[:, t - (CONV_B - 1):, :]
    else:
        conv_state = jnp.concatenate([cache["conv"], qkv_b], axis=1)[:, -(CONV_B - 1):, :]

    tm2 = _pick(m, (512, 256, 128, 64))
    pre1 = _matmul([out_a.reshape(m, w_a), o_b.reshape(m, w_a)], w["w_o"], tm=tm2, tn=512,
                   out_dtype=F32, resid=xn, alpha=w["alpha"])
    x1, x1b = _layer_norm(pre1, w["ln1_g"], w["ln1_b"], (F32, BF16))

    f2 = w["w_ffn_up"].shape[1]
    tn_f = _pick(f2 // 2, (256, 128))
    if cache is None:
        act = _ffn_up(x1b, w["w_ffn_up"], w["ffn_conv_w"], w["ffn_conv_b"], None, seg=t, tm=min(tm, t), tn=tn_f)
    else:
        tm_s = _pick(m, (1024, 512)) if m >= 512 else m
        assert tm_s % t == 0 and (tm_s // t) % SUBLANES == 0 or tm_s == m
        act = _ffn_up(x1b, w["w_ffn_up"], w["ffn_conv_w"], w["ffn_conv_b"], cache["ffn"], seg=t, tm=tm_s,
                      tn=tn_f)
    pre2 = _matmul([act], w["w_ffn_down"], tm=tm2, tn=256, out_dtype=F32, resid=x1, alpha=w["alpha"])
    (y,) = _layer_norm(pre2, w["ln2_g"], w["ln2_b"], (F32,))

    x1b3 = x1b.reshape(b, t, d)
    tail = x1b3[:, t - (FFN_CONV - 1):, :].reshape(b * (FFN_CONV - 1), d)
    states = (k_new.reshape(b, t, N_KV_A, HEAD_DIM), v_new.reshape(b, t, N_KV_A, HEAD_DIM), ik_new,
              s_new, conv_state)
    return y, x1, states, tail


def _prep_weights(l, depth, w_in, conv_qkv_w, conv_qkv_b, a_log, dt_bias, delta_norm_g, rel_bias, w_o,
                  ln1_g, ln1_b, w_ffn_up, ffn_conv_w, ffn_conv_b, w_ffn_down, ln2_g, ln2_b):
    d = w_in.shape[1]
    n_heads = d // (2 * HEAD_DIM)
    w_a = n_heads * HEAD_DIM
    w_kv = N_KV_A * HEAD_DIM
    w_iq = IDX_HEADS * IDX_DIM
    c_qkv = 3 * w_a
    sizes = (w_a, w_kv, w_kv, w_iq, IDX_DIM, IDX_HEADS, c_qkv, w_a, n_heads, n_heads)
    offs = np.concatenate([[0], np.cumsum(sizes)])
    assert offs[-1] == w_in.shape[2] and n_heads == 16
    wi = w_in[l]
    seg = lambda a, b_: wi[:, offs[a]:offs[b_]]
    w_main = jnp.concatenate([seg(0, 4), seg(6, 8)], axis=1).astype(BF16)
    pad = LANES - (IDX_DIM + IDX_HEADS + 2 * n_heads)
    w_small = jnp.concatenate([seg(4, 6), seg(8, 10), jnp.zeros((d, pad), F32)], axis=1).astype(BF16)
    lane_pad = lambda v: jnp.pad(v.astype(F32), (SM_A, LANES - SM_A - n_heads)).reshape(1, LANES)
    return {
        "n_heads": n_heads,
        "alpha": float((2 * depth) ** 0.25),
        "w_main": w_main, "w_small": w_small,
        "conv_qkv_w": conv_qkv_w[l], "conv_qkv_b": conv_qkv_b[l].reshape(1, -1),
        "alog_row": lane_pad(a_log[l]), "dtb_row": lane_pad(dt_bias[l]),
        "norm_g": delta_norm_g[l].reshape(1, -1),
        "rel_bias": rel_bias,
        "w_o": w_o[l].astype(BF16),
        "ln1_g": ln1_g[l], "ln1_b": ln1_b[l],
        "w_ffn_up": w_ffn_up[l].astype(BF16),
        "ffn_conv_w": ffn_conv_w[l], "ffn_conv_b": ffn_conv_b[l].reshape(1, -1),
        "w_ffn_down": w_ffn_down[l].astype(BF16),
        "ln2_g": ln2_g[l], "ln2_b": ln2_b[l],
    }


def kernel(x_prompt, x_sample, cache_attn_k, cache_attn_v, cache_idx_k, state_delta, state_conv_qkv,
           state_ffn_conv, ln_in_g, ln_in_b, w_in, conv_qkv_w, conv_qkv_b, a_log, dt_bias, delta_norm_g,
           rel_bias, w_o, ln1_g, ln1_b, w_ffn_up, ffn_conv_w, ffn_conv_b, w_ffn_down, ln2_g, ln2_b):
    bp, tp, d = x_prompt.shape
    bs, ts, _ = x_sample.shape
    depth = w_in.shape[0]
    xp, xpb = _layer_norm(x_prompt.reshape(bp * tp, d), ln_in_g, ln_in_b, (F32, BF16))
    xs, xsb = _layer_norm(x_sample.reshape(bs * ts, d), ln_in_g, ln_in_b, (F32, BF16))
    p_states, s_states = [], []
    for l in range(depth):
        w = _prep_weights(l, depth, w_in, conv_qkv_w, conv_qkv_b, a_log, dt_bias, delta_norm_g, rel_bias,
                          w_o, ln1_g, ln1_b, w_ffn_up, ffn_conv_w, ffn_conv_b, w_ffn_down, ln2_g, ln2_b)
        cache = {"k": cache_attn_k[l], "v": cache_attn_v[l], "ik": cache_idx_k[l],
                 "delta": state_delta[l], "conv": state_conv_qkv[l], "ffn": state_ffn_conv[l]}
        xp, xp1, st_p, tail_p = _layer(xp, xpb, None, w, b=bp, t=tp)
        xs, xs1, st_s, tail_s = _layer(xs, xsb, cache, w, b=bs, t=ts)
        tail = jnp.concatenate([tail_p, tail_s], axis=0)
        ffn_state = _matmul([tail], w["w_ffn_up"], tm=tail.shape[0], tn=_pick(w["w_ffn_up"].shape[1], (512, 256, 128)),
                            out_dtype=F32)
        n_p = bp * (FFN_CONV - 1)
        p_states.append(st_p + (ffn_state[:n_p].reshape(bp, FFN_CONV - 1, -1),))
        s_states.append(st_s + (ffn_state[n_p:].reshape(bs, FFN_CONV - 1, -1),))
        if l + 1 < depth:
            xpb, xsb = xp.astype(BF16), xs.astype(BF16)
    p_out = [jnp.stack(z) for z in zip(*p_states)]
    s_out = [jnp.stack(z) for z in zip(*s_states)]
    return (xp.reshape(bp, tp, d), xs.reshape(bs, ts, d), *p_out, *s_out)
```

```python
import functools
import math

import numpy as np
import jax
import jax.numpy as jnp
from jax import lax
from jax.experimental import pallas as pl
from jax.experimental.pallas import tpu as pltpu

F32 = jnp.float32
BF16 = jnp.bfloat16

CHUNK = 64
HEAD_DIM = 128
N_KV_A = 4
IDX_HEADS = 16
IDX_DIM = 64
TOPK_MAX = 256
CONV_B = 4
FFN_CONV = 3
N_BUCKETS = 32
MAX_DIST = 128
EPS = 1e-5
IDX_SCALE = (IDX_HEADS ** -0.5) * (IDX_DIM ** -0.5)

LANES = 128
SUBLANES = 8
BF16_ROWS = 16
VMEM_LIMIT = 56 * 1024 * 1024

KEY_BLOCK = LANES
GDN_CHUNK = LANES
INV_BASE = 16
NEG = -1e30
INT_MIN = -2 ** 31

NT_DIMS = (((1,), (1,)), ((), ()))

SM_IK = 0
SM_IW = IDX_DIM
SM_BETA = SM_IW + IDX_HEADS
SM_A = SM_BETA + 16


def _params(n_grid_axes):
    return pltpu.CompilerParams(dimension_semantics=("arbitrary",) * n_grid_axes,
                                vmem_limit_bytes=VMEM_LIMIT)


def _dot(a, b, dims=None):
    a = a.astype(BF16)
    b = b.astype(BF16)
    if dims is None:
        return jnp.dot(a, b, preferred_element_type=F32)
    return lax.dot_general(a, b, dims, preferred_element_type=F32)


def _dot_exact(a, b):
    return jnp.dot(a, b, preferred_element_type=F32, precision=lax.Precision.HIGHEST)


def _silu(x):
    return x * (1.0 / (1.0 + jnp.exp(-x)))


def _ln_kernel(x_ref, g_ref, b_ref, *out_refs):
    x = x_ref[...]
    mu = jnp.mean(x, axis=-1, keepdims=True)
    xc = x - mu
    var = jnp.mean(xc * xc, axis=-1, keepdims=True)
    y = xc * lax.rsqrt(var + EPS) * g_ref[...] + b_ref[...]
    for o_ref in out_refs:
        o_ref[...] = y.astype(o_ref.dtype)


def _layer_norm(x, g, b, out_dtypes):
    m, d = x.shape
    tr = _pick(m, (256, 128, 64))
    row = pl.BlockSpec((tr, d), lambda i: (i, 0))
    vec = pl.BlockSpec((1, d), lambda i: (0, 0))
    outs = pl.pallas_call(
        _ln_kernel,
        out_shape=[jax.ShapeDtypeStruct((m, d), dt) for dt in out_dtypes],
        grid=(m // tr,),
        in_specs=[row, vec, vec],
        out_specs=[row for _ in out_dtypes],
        compiler_params=_params(1),
        name="layer_norm",
    )(x, g.reshape(1, d), b.reshape(1, d))
    return outs


def _mm_kernel(*refs, k_sizes, alpha, has_resid):
    n_lhs = len(k_sizes)
    w_ref = refs[n_lhs]
    o_ref = refs[-1]
    acc = None
    off = 0
    for a_ref, kp in zip(refs[:n_lhs], k_sizes):
        d = jnp.dot(a_ref[...], w_ref[off:off + kp, :], preferred_element_type=F32)
        acc = d if acc is None else acc + d
        off += kp
    if has_resid:
        acc = acc + alpha * refs[n_lhs + 1][...]
    o_ref[...] = acc.astype(o_ref.dtype)


def _matmul(lhs_list, w, *, tm, tn, out_dtype, resid=None, alpha=1.0):
    m = lhs_list[0].shape[0]
    k_sizes = tuple(a.shape[1] for a in lhs_list)
    k, n = w.shape
    assert sum(k_sizes) == k and m % tm == 0 and n % tn == 0
    in_specs = [pl.BlockSpec((tm, kp), lambda i, j: (i, 0)) for kp in k_sizes]
    in_specs.append(pl.BlockSpec((k, tn), lambda i, j: (0, j)))
    args = list(lhs_list) + [w]
    if resid is not None:
        in_specs.append(pl.BlockSpec((tm, tn), lambda i, j: (i, j)))
        args.append(resid)
    return pl.pallas_call(
        functools.partial(_mm_kernel, k_sizes=k_sizes, alpha=alpha, has_resid=resid is not None),
        out_shape=jax.ShapeDtypeStruct((m, n), out_dtype),
        grid=(m // tm, n // tn),
        in_specs=in_specs,
        out_specs=pl.BlockSpec((tm, tn), lambda i, j: (i, j)),
        compiler_params=_params(2),
        name="matmul",
    )(*args)


def _t5_bucket_np(rel):
    half = N_BUCKETS // 2
    max_exact = half // 2
    n = np.abs(rel)
    large = max_exact + np.floor(2.0 * np.log2(np.maximum(n, 1) / max_exact) + 1e-9).astype(np.int64)
    large = np.minimum(large, half - 1)
    return np.where(rel > 0, half, 0) + np.where(n < max_exact, n, large)


def _bias_tiles(rel_bias, tq):
    n_heads = rel_bias.shape[1]
    group = n_heads // N_KV_A
    i = np.arange(tq)[:, None]
    j = np.arange(KEY_BLOCK)[None, :]
    tiles = []
    for r in range(3):
        rel = -r * KEY_BLOCK + j - i
        bucket = _t5_bucket_np(rel)
        if r == 2:
            assert (bucket == N_BUCKETS // 2 - 1).all()
        t = rel_bias[bucket.reshape(-1)].reshape(tq, KEY_BLOCK, N_KV_A, group)
        tiles.append(jnp.transpose(t, (2, 3, 0, 1)).reshape(N_KV_A, group * tq, KEY_BLOCK))
    return jnp.stack(tiles).astype(F32)


def _sortable(x):
    i = lax.bitcast_convert_type(x, jnp.int32)
    return jnp.where(i >= 0, i, i ^ jnp.int32(0x7FFFFFFF))


def _dsa_kernel(q_ref, iq_ref, sm_ref, k_ref, v_ref, ik_ref, bias_ref, o_ref,
                key_scr, mask_scr, logit_scr, m_scr, l_scr, acc_scr, *, tq, qb0, top_k, group):
    qb = pl.program_id(1) + qb0
    nkb = qb + 1
    rows = group * tq

    row = lax.broadcasted_iota(jnp.int32, (tq, KEY_BLOCK), 0)
    lane = lax.broadcasted_iota(jnp.int32, (tq, KEY_BLOCK), 1)
    limit = ((qb * KEY_BLOCK + row) // CHUNK + 1) * CHUNK
    lane_k = lax.broadcasted_iota(jnp.int32, (KEY_BLOCK, LANES), 1)

    iw = sm_ref[:, SM_IW:SM_IW + IDX_HEADS] * IDX_SCALE
    iqb = iq_ref[...].astype(BF16)

    def idx_body(r, carry):
        kb = qb - r
        ik = ik_ref[pl.ds(pl.multiple_of(kb * KEY_BLOCK, KEY_BLOCK), KEY_BLOCK), :]
        ik_lo = jnp.where(lane_k < IDX_DIM, ik, 0.0)
        ik_hi = pltpu.roll(ik_lo, IDX_DIM, axis=1)
        rhs = jnp.concatenate([ik_lo, ik_hi], axis=0).astype(BF16)
        acc = jnp.zeros((tq, KEY_BLOCK), F32)
        for g in range(IDX_HEADS // 2):
            s = lax.dot_general(iqb[:, g * LANES:(g + 1) * LANES], rhs, NT_DIMS,
                                preferred_element_type=F32)
            acc = acc + jnp.maximum(s[:, :KEY_BLOCK], 0.0) * iw[:, 2 * g:2 * g + 1]
            acc = acc + jnp.maximum(s[:, KEY_BLOCK:], 0.0) * iw[:, 2 * g + 1:2 * g + 2]
        admissible = kb * KEY_BLOCK + lane < limit
        key_scr[r] = jnp.where(admissible, _sortable(acc), INT_MIN)
        return carry

    lax.fori_loop(0, nkb, idx_body, 0)

    def count_ge(cand):
        cand_b = jnp.broadcast_to(cand, (tq, KEY_BLOCK))

        def body(r, c):
            return c + jnp.where(key_scr[r] >= cand_b, 1.0, 0.0)

        c = lax.fori_loop(0, nkb, body, jnp.zeros((tq, KEY_BLOCK), F32))
        return jnp.sum(c, axis=1, keepdims=True)

    kf = float(top_k)
    zero = jnp.zeros((tq, 1), jnp.int32)
    thr = jnp.where(count_ge(zero) >= kf, zero, jnp.int32(INT_MIN))

    def bit_body(it, thr):
        cand = thr | jnp.left_shift(jnp.int32(1), 30 - it)
        return jnp.where(count_ge(cand) >= kf, cand, thr)

    thr = lax.fori_loop(0, 31, bit_body, thr)
    thr_b = jnp.broadcast_to(thr, (tq, KEY_BLOCK))

    def mask_body(r, carry):
        key = key_scr[r]
        sel = (key >= thr_b) & (key != INT_MIN)
        m = jnp.where(sel, 0.0, NEG)
        mask_scr[r] = jnp.concatenate([m] * group, axis=0)
        return carry

    lax.fori_loop(0, nkb, mask_body, 0)

    scale = HEAD_DIM ** -0.5
    for n in range(N_KV_A):
        q4 = jnp.concatenate(
            [q_ref[:, (group * n + g) * HEAD_DIM:(group * n + g + 1) * HEAD_DIM] for g in range(group)],
            axis=0).astype(BF16)
        m_scr[...] = jnp.full((rows, KEY_BLOCK), NEG, F32)

        def logits_body(r, carry):
            kb = qb - r
            kblk = k_ref[pl.ds(pl.multiple_of(kb * KEY_BLOCK, KEY_BLOCK), KEY_BLOCK),
                         n * HEAD_DIM:(n + 1) * HEAD_DIM]
            s = lax.dot_general(q4, kblk.astype(BF16), NT_DIMS, preferred_element_type=F32) * scale
            s = s + bias_ref[jnp.minimum(r, 2), n] + mask_scr[r]
            logit_scr[r] = s
            m_scr[...] = jnp.maximum(m_scr[...], s)
            return carry

        lax.fori_loop(0, nkb, logits_body, 0)
        m = jnp.max(m_scr[...], axis=1, keepdims=True)
        l_scr[...] = jnp.zeros((rows, KEY_BLOCK), F32)
        acc_scr[...] = jnp.zeros((rows, HEAD_DIM), F32)

        def pv_body(r, carry):
            kb = qb - r
            p = jnp.exp(logit_scr[r] - m)
            vblk = v_ref[pl.ds(pl.multiple_of(kb * KEY_BLOCK, KEY_BLOCK), KEY_BLOCK),
                         n * HEAD_DIM:(n + 1) * HEAD_DIM]
            l_scr[...] += p
            acc_scr[...] += jnp.dot(p.astype(BF16), vblk.astype(BF16), preferred_element_type=F32)
            return carry

        lax.fori_loop(0, nkb, pv_body, 0)
        out = acc_scr[...] / jnp.sum(l_scr[...], axis=1, keepdims=True)
        for g in range(group):
            h = group * n + g
            o_ref[:, h * HEAD_DIM:(h + 1) * HEAD_DIM] = out[g * tq:(g + 1) * tq].astype(o_ref.dtype)


def _dsa(q_src, q_blk, iq_src, iq_blk, small, k_src, k_blk, v_src, v_blk, ik_src, bias, *,
         t, tq, qb0, top_k, n_heads):
    b = q_src.shape[0]
    l = k_src.shape[1]
    group = n_heads // N_KV_A
    nq = t // tq
    nkb_max = qb0 + nq
    assert l == nkb_max * KEY_BLOCK and t % tq == 0 and tq % CHUNK == 0 and tq <= KEY_BLOCK
    assert tq == KEY_BLOCK or nq == 1
    w_a = n_heads * HEAD_DIM
    w_kv = N_KV_A * HEAD_DIM
    w_iq = IDX_HEADS * IDX_DIM
    rows = group * tq
    kern = functools.partial(_dsa_kernel, tq=tq, qb0=qb0, top_k=top_k, group=group)
    return pl.pallas_call(
        kern,
        out_shape=jax.ShapeDtypeStruct((b, t, w_a), BF16),
        grid=(b, nq),
        in_specs=[
            pl.BlockSpec((None, tq, w_a), lambda bi, i: (bi, i, q_blk)),
            pl.BlockSpec((None, tq, w_iq), lambda bi, i: (bi, i, iq_blk)),
            pl.BlockSpec((None, tq, LANES), lambda bi, i: (bi, i, 0)),
            pl.BlockSpec((None, l, w_kv), lambda bi, i: (bi, 0, k_blk)),
            pl.BlockSpec((None, l, w_kv), lambda bi, i: (bi, 0, v_blk)),
            pl.BlockSpec((None, l, LANES), lambda bi, i: (bi, 0, 0)),
            pl.BlockSpec((3, N_KV_A, rows, KEY_BLOCK), lambda bi, i: (0, 0, 0, 0)),
        ],
        out_specs=pl.BlockSpec((None, tq, w_a), lambda bi, i: (bi, i, 0)),
        scratch_shapes=[
            pltpu.VMEM((nkb_max, tq, KEY_BLOCK), jnp.int32),
            pltpu.VMEM((nkb_max, rows, KEY_BLOCK), F32),
            pltpu.VMEM((nkb_max, rows, KEY_BLOCK), F32),
            pltpu.VMEM((rows, KEY_BLOCK), F32),
            pltpu.VMEM((rows, KEY_BLOCK), F32),
            pltpu.VMEM((rows, HEAD_DIM), F32),
        ],
        compiler_params=_params(2),
        name="dsa",
    )(q_src, iq_src, small, k_src, v_src, ik_src, bias)


def _gdn_kernel(xq_ref, xk_ref, xv_ref, z_ref, sm_ref, pq_ref, pk_ref, pv_ref,
                wq_ref, wk_ref, wv_ref, bq_ref, bk_ref, bv_ref, alog_ref, dtb_ref, ng_ref, s0_ref,
                o_ref, s_ref, gc_scr, gt_scr, u_scr, w_scr, qd_scr, kdt_scr, qk_scr, gl_scr, *, t, hb, c):
    hg = pl.program_id(1)
    n_chunks = t // c

    ri = lax.broadcasted_iota(jnp.int32, (c, c), 0)
    ci = lax.broadcasted_iota(jnp.int32, (c, c), 1)
    tri = ri >= ci
    strict = ri > ci
    tril = jnp.where(tri, 1.0, 0.0)
    eye = jnp.where(ri == ci, 1.0, 0.0)
    level_masks = [ri // INV_BASE == ci // INV_BASE]
    size = INV_BASE
    while size < c:
        level_masks.append((ri // (2 * size) == ci // (2 * size)) & (ri // size != ci // size))
        size *= 2
    a_scale = -jnp.exp(alog_ref[...])
    dtb = dtb_ref[...]
    for bi in range(n_chunks):
        a = sm_ref[bi * c:(bi + 1) * c, :] + dtb
        g = a_scale * (jnp.maximum(a, 0.0) + jnp.log1p(jnp.exp(-jnp.abs(a))))
        gc = _dot_exact(tril, g)
        gc_scr[bi * c:(bi + 1) * c, :] = gc
        if c < LANES:
            gc = jnp.concatenate([gc, jnp.zeros((LANES - c, LANES), F32)], axis=0)
        gt_scr[bi] = gc.T[:, :c]

    lane = lax.broadcasted_iota(jnp.int32, (c, LANES), 1)
    ng = ng_ref[...]

    def conv(win, w_ref, b_ref, hh):
        cols = slice(hh * HEAD_DIM, (hh + 1) * HEAD_DIM)
        y = b_ref[:, cols]
        for j in range(CONV_B):
            lo = SUBLANES - (CONV_B - 1) + j
            y = y + win[lo:lo + c, cols] * w_ref[j:j + 1, cols]
        return _silu(y)

    def l2n(x):
        return x * lax.rsqrt(jnp.sum(x * x, axis=-1, keepdims=True) + 1e-6)

    def window(x_ref, p_ref, ck, row0):
        halo = x_ref[pl.ds(pl.multiple_of(jnp.maximum(row0 - SUBLANES, 0), SUBLANES), SUBLANES), :]
        halo = jnp.where(ck == 0, p_ref[...], halo)
        return jnp.concatenate([halo, x_ref[pl.ds(row0, c), :]], axis=0)

    cpi = 2 if n_chunks % 2 == 0 else 1

    def prep(it, carry):
        chains = []
        for sub in range(cpi):
            ck = it * cpi + sub
            row0 = pl.multiple_of(ck * c, c)
            wq = window(xq_ref, pq_ref, ck, row0)
            wk = window(xk_ref, pk_ref, ck, row0)
            wv = window(xv_ref, pv_ref, ck, row0)
            beta_all = 1.0 / (1.0 + jnp.exp(-sm_ref[pl.ds(row0, c), :]))
            gc_all = gc_scr[pl.ds(row0, c), :]
            for hh in range(hb):
                h = hg * hb + hh
                ch = {"ck": ck, "row0": row0, "hh": hh}
                ch["beta"] = jnp.sum(jnp.where(lane == SM_BETA + h, beta_all, 0.0), axis=1, keepdims=True)
                gcol = jnp.sum(jnp.where(lane == SM_A + h, gc_all, 0.0), axis=1, keepdims=True)
                grow = gt_scr[ck, pl.ds(SM_A + h, 1), :]
                ch["glast"] = gcol[c - 1:c, :]
                ch["gcol"] = gcol
                ch["dmask"] = jnp.where(tri, jnp.exp(jnp.where(tri, gcol - grow, 0.0)), 0.0)
                ch["q"] = l2n(conv(wq, wq_ref, bq_ref, hh)) * (HEAD_DIM ** -0.5)
                ch["k"] = l2n(conv(wk, wk_ref, bk_ref, hh))
                ch["v"] = conv(wv, wv_ref, bv_ref, hh)
                ch["kb"] = ch["k"] * ch["beta"]
                chains.append(ch)
        for ch in chains:
            ch["kk"] = _dot(ch["kb"], ch["k"], NT_DIMS)
            ch["qk"] = _dot(ch["q"], ch["k"], NT_DIMS)
        for ch in chains:
            ch["lmat"] = jnp.where(strict, ch["kk"] * ch["dmask"], 0.0)
            ch["npow"] = jnp.where(level_masks[0], -ch["lmat"], 0.0)
            ch["ainv"] = eye + ch["npow"]
        for _ in range(int(math.log2(INV_BASE)) - 1):
            for ch in chains:
                ch["npow"] = _dot(ch["npow"], ch["npow"])
            for ch in chains:
                ch["ainv"] = ch["ainv"] + _dot(ch["ainv"], ch["npow"])
        for lm in level_masks[1:]:
            for ch in chains:
                ch["tmp"] = _dot(ch["ainv"], jnp.where(lm, ch["lmat"], 0.0))
            for ch in chains:
                ch["ainv"] = ch["ainv"] - _dot(ch["tmp"], ch["ainv"])
        for ch in chains:
            ch["eg"] = jnp.exp(ch["gcol"])
            ch["sol"] = _dot(ch["ainv"], jnp.concatenate([ch["v"] * ch["beta"], ch["kb"] * ch["eg"]], axis=1))
        for ch in chains:
            ck, row0, hh = ch["ck"], ch["row0"], ch["hh"]
            qk = jnp.where(tri, ch["qk"] * ch["dmask"], 0.0)
            k_dec = ch["k"] * jnp.exp(ch["glast"] - ch["gcol"])
            u_scr[hh, pl.ds(row0, c), :] = ch["sol"][:, :HEAD_DIM]
            w_scr[hh, pl.ds(row0, c), :] = ch["sol"][:, HEAD_DIM:].astype(BF16)
            qd_scr[hh, pl.ds(row0, c), :] = (ch["q"] * ch["eg"]).astype(BF16)
            kdt_scr[hh, ck] = k_dec.T.astype(BF16) if c == LANES else jnp.concatenate(
                [k_dec, jnp.zeros((LANES - c, HEAD_DIM), F32)], axis=0).T[:, :c].astype(BF16)
            qk_scr[hh, ck] = qk.astype(BF16)
            gl_scr[hh, pl.ds(ck, 1), :] = jnp.broadcast_to(jnp.exp(ch["glast"]), (1, LANES))
        return carry

    lax.fori_loop(0, n_chunks // cpi, prep, 0)

    def scan(ck, states):
        row0 = pl.multiple_of(ck * c, c)
        new_states = []
        for hh in range(hb):
            cols = slice(hh * HEAD_DIM, (hh + 1) * HEAD_DIM)
            s = states[hh]
            sb = s.astype(BF16)
            v_new = u_scr[hh, pl.ds(row0, c), :] - jnp.dot(w_scr[hh, pl.ds(row0, c), :], sb,
                                                         preferred_element_type=F32)
            vb = v_new.astype(BF16)
            o = jnp.dot(qd_scr[hh, pl.ds(row0, c), :], sb, preferred_element_type=F32)
            o = o + jnp.dot(qk_scr[hh, ck], vb, preferred_element_type=F32)
            s = s * gl_scr[hh, pl.ds(ck, 1), :] + jnp.dot(kdt_scr[hh, ck], vb, preferred_element_type=F32)
            new_states.append(s)
            o = o * lax.rsqrt(jnp.mean(o * o, axis=-1, keepdims=True) + EPS) * ng
            o = o * _silu(z_ref[pl.ds(row0, c), cols])
            o_ref[pl.ds(row0, c), cols] = o.astype(o_ref.dtype)
        return tuple(new_states)

    states = lax.fori_loop(0, n_chunks, scan, tuple(s0_ref[hh] for hh in range(hb)))
    for hh in range(hb):
        s_ref[hh] = states[hh]


def _gdn(h_main, qkv_col0, z_col0, small, prev8, conv_w, conv_b, alog_row, dtb_row, norm_g, s0, *, hb):
    b, t, _ = h_main.shape
    n_heads = s0.shape[1]
    w_b = n_heads * HEAD_DIM
    hw = hb * HEAD_DIM
    ng = n_heads // hb
    c = GDN_CHUNK if t % GDN_CHUNK == 0 else CHUNK
    n_chunks = t // c
    assert t % c == 0 and n_heads % hb == 0

    def col(c0):
        return lambda bi, g: (bi, 0, c0 // hw + g)

    def wcol(c0):
        return lambda bi, g: (0, c0 // hw + g)

    tok = lambda c0: pl.BlockSpec((None, t, hw), col(c0))
    prev = lambda c0: pl.BlockSpec((None, SUBLANES, hw), col(c0))
    cw = lambda c0: pl.BlockSpec((CONV_B, hw), wcol(c0))
    cb = lambda c0: pl.BlockSpec((1, hw), wcol(c0))
    vec = pl.BlockSpec((1, LANES), lambda bi, g: (0, 0))
    state = pl.BlockSpec((None, hb, HEAD_DIM, HEAD_DIM), lambda bi, g: (bi, g, 0, 0))
    return pl.pallas_call(
        functools.partial(_gdn_kernel, t=t, hb=hb, c=c),
        out_shape=[jax.ShapeDtypeStruct((b, t, w_b), BF16),
                   jax.ShapeDtypeStruct(s0.shape, F32)],
        grid=(b, ng),
        in_specs=[tok(qkv_col0), tok(qkv_col0 + w_b), tok(qkv_col0 + 2 * w_b), tok(z_col0),
                  pl.BlockSpec((None, t, LANES), lambda bi, g: (bi, 0, 0)),
                  prev(0), prev(w_b), prev(2 * w_b),
                  cw(0), cw(w_b), cw(2 * w_b), cb(0), cb(w_b), cb(2 * w_b),
                  vec, vec, vec, state],
        out_specs=[pl.BlockSpec((None, t, hw), lambda bi, g: (bi, 0, g)), state],
        scratch_shapes=[pltpu.VMEM((t, LANES), F32),
                        pltpu.VMEM((n_chunks, LANES, c), F32),
                        pltpu.VMEM((hb, t, HEAD_DIM), F32),
                        pltpu.VMEM((hb, t, HEAD_DIM), BF16),
                        pltpu.VMEM((hb, t, HEAD_DIM), BF16),
                        pltpu.VMEM((hb, n_chunks, HEAD_DIM, c), BF16),
                        pltpu.VMEM((hb, n_chunks, c, c), BF16),
                        pltpu.VMEM((hb, max(n_chunks, SUBLANES), LANES), F32)],
        compiler_params=_params(2),
        name="gdn",
    )(h_main, h_main, h_main, h_main, small, prev8, prev8, prev8,
      conv_w, conv_w, conv_w, conv_b, conv_b, conv_b, alog_row, dtb_row, norm_g, s0)


def _ffn_up_kernel(*refs, tm, seg, has_prev):
    x_ref, halo_ref, wg_ref, wv_ref, cwg_ref, cwv_ref, cbg_ref, cbv_ref = refs[:8]
    o_ref = refs[-1]
    i = pl.program_id(0)
    tn = o_ref.shape[1]
    xe = jnp.concatenate([halo_ref[...], x_ref[...]], axis=0)
    r = (i * tm + lax.broadcasted_iota(jnp.int32, (tm, tn), 0)) % seg

    if has_prev:
        nseg = tm // seg
        srow = lax.broadcasted_iota(jnp.int32, (tm, nseg), 0)
        scol = lax.broadcasted_iota(jnp.int32, (tm, nseg), 1) * seg
        place0 = jnp.where(srow == scol, 1.0, 0.0)
        place1 = jnp.where(srow == scol + 1, 1.0, 0.0)

    def branch(w_ref, cw_ref, cb_ref, prev_refs):
        up = jnp.dot(xe, w_ref[...], preferred_element_type=F32)
        y = cb_ref[...] + up[BF16_ROWS:, :] * cw_ref[2:3, :]
        y = y + jnp.where(r >= 1, up[BF16_ROWS - 1:BF16_ROWS - 1 + tm, :], 0.0) * cw_ref[1:2, :]
        y = y + jnp.where(r >= 2, up[BF16_ROWS - 2:BF16_ROWS - 2 + tm, :], 0.0) * cw_ref[0:1, :]
        if has_prev:
            p0, p1 = prev_refs[0][...], prev_refs[1][...]
            y = y + _dot_exact(place0, p0 * cw_ref[0:1, :] + p1 * cw_ref[1:2, :])
            y = y + _dot_exact(place1, p1 * cw_ref[0:1, :])
        return y

    gate = branch(wg_ref, cwg_ref, cbg_ref, refs[8:10])
    val = branch(wv_ref, cwv_ref, cbv_ref, refs[10:12])
    o_ref[...] = (_silu(gate) * val).astype(o_ref.dtype)


def _ffn_up(x1b, w_up, conv_w, conv_b, prev, *, seg, tm, tn):
    m, d = x1b.shape
    f = w_up.shape[1] // 2
    nj = f // tn
    has_prev = prev is not None
    assert m % tm == 0 and f % tn == 0 and (seg % tm == 0 or tm % seg == 0) and tm % BF16_ROWS == 0
    hrows = tm // BF16_ROWS
    in_specs = [
        pl.BlockSpec((tm, d), lambda i, j: (i, 0)),
        pl.BlockSpec((BF16_ROWS, d), lambda i, j: (jnp.maximum(i * hrows - 1, 0), 0)),
        pl.BlockSpec((d, tn), lambda i, j: (0, j)),
        pl.BlockSpec((d, tn), lambda i, j: (0, nj + j)),
        pl.BlockSpec((FFN_CONV, tn), lambda i, j: (0, j)),
        pl.BlockSpec((FFN_CONV, tn), lambda i, j: (0, nj + j)),
        pl.BlockSpec((1, tn), lambda i, j: (0, j)),
        pl.BlockSpec((1, tn), lambda i, j: (0, nj + j)),
    ]
    args = [x1b, x1b, w_up, w_up, conv_w, conv_w, conv_b, conv_b]
    if has_prev:
        nseg = tm // seg
        p0, p1 = prev[:, 0, :], prev[:, 1, :]
        in_specs += [pl.BlockSpec((nseg, tn), lambda i, j: (i, j)),
                     pl.BlockSpec((nseg, tn), lambda i, j: (i, j)),
                     pl.BlockSpec((nseg, tn), lambda i, j: (i, nj + j)),
                     pl.BlockSpec((nseg, tn), lambda i, j: (i, nj + j))]
        args += [p0, p1, p0, p1]
    return pl.pallas_call(
        functools.partial(_ffn_up_kernel, tm=tm, seg=seg, has_prev=has_prev),
        out_shape=jax.ShapeDtypeStruct((m, f), BF16),
        grid=(m // tm, nj),
        in_specs=in_specs,
        out_specs=pl.BlockSpec((tm, tn), lambda i, j: (i, j)),
        compiler_params=_params(2),
        name="ffn_up",
    )(*args)


def _pick(n, candidates):
    for c in candidates:
        if n % c == 0:
            return c
    raise ValueError(f"no tile for {n}")


def _layer(xn, xnb, cache, w, *, b, t):
    m, d = xn.shape
    n_heads = w["n_heads"]
    w_a = n_heads * HEAD_DIM
    w_kv = N_KV_A * HEAD_DIM
    w_iq = IDX_HEADS * IDX_DIM
    c_qkv = 3 * w_a
    tm = _pick(m, (1024, 512, 256, 128, 64))

    h_main = _matmul([xnb], w["w_main"], tm=tm, tn=512, out_dtype=F32)
    small = _matmul([xnb], w["w_small"], tm=tm, tn=LANES, out_dtype=F32)
    h3 = h_main.reshape(b, t, -1)
    small3 = small.reshape(b, t, LANES)
    col_k, col_v, col_iq, col_qkv, col_z = w_a, w_a + w_kv, w_a + 2 * w_kv, w_a + 2 * w_kv + w_iq, None
    col_z = col_qkv + c_qkv
    k_new = h3[:, :, col_k:col_k + w_kv]
    v_new = h3[:, :, col_v:col_v + w_kv]
    ik_new = small3[:, :, SM_IK:SM_IK + IDX_DIM]
    qkv_b = h3[:, :, col_qkv:col_qkv + c_qkv]

    if cache is None:
        past = 0
        tq = KEY_BLOCK
        assert t % KEY_BLOCK == 0
        k_src, k_blk, v_src, v_blk, ik_src = h3, col_k // w_kv, h3, col_v // w_kv, small3
    else:
        past = cache["k"].shape[1]
        tq = t
        assert past % KEY_BLOCK == 0 and t <= KEY_BLOCK
        pad = KEY_BLOCK - t
        k_src = jnp.concatenate([cache["k"].reshape(b, past, w_kv), k_new,
                                 jnp.zeros((b, pad, w_kv), F32)], axis=1)
        v_src = jnp.concatenate([cache["v"].reshape(b, past, w_kv), v_new,
                                 jnp.zeros((b, pad, w_kv), F32)], axis=1)
        ik_all = jnp.concatenate([cache["ik"], ik_new, jnp.zeros((b, pad, IDX_DIM), F32)], axis=1)
        ik_src = jnp.pad(ik_all, ((0, 0), (0, 0), (0, LANES - IDX_DIM)))
        k_blk = v_blk = 0
    top_k = min(TOPK_MAX, (past + t) // 4)
    out_a = _dsa(h3, 0, h3, col_iq // w_iq, small3, k_src, k_blk, v_src, v_blk, ik_src,
                 _bias_tiles(w["rel_bias"], tq), t=t, tq=tq, qb0=past // KEY_BLOCK, top_k=top_k,
                 n_heads=n_heads)

    if cache is None:
        prev8 = jnp.zeros((b, SUBLANES, c_qkv), F32)
        s0 = jnp.zeros((b, n_heads, HEAD_DIM, HEAD_DIM), F32)
    else:
        prev8 = jnp.pad(cache["conv"], ((0, 0), (SUBLANES - (CONV_B - 1), 0), (0, 0)))
        s0 = cache["delta"]
    o_b, s_new = _gdn(h3, col_qkv, col_z, small3, prev8, w["conv_qkv_w"], w["conv_qkv_b"],
                      w["alog_row"], w["dtb_row"], w["norm_g"], s0, hb=2 if cache is None else 8)
    if t >= CONV_B - 1:
        conv_state = qkv_b[:, t - (CONV_B - 1):, :]
    else:
        conv_state = jnp.concatenate([cache["conv"], qkv_b], axis=1)[:, -(CONV_B - 1):, :]

    tm2 = _pick(m, (512, 256, 128, 64))
    pre1 = _matmul([out_a.reshape(m, w_a), o_b.reshape(m, w_a)], w["w_o"], tm=tm2, tn=512,
                   out_dtype=F32, resid=xn, alpha=w["alpha"])
    x1, x1b = _layer_norm(pre1, w["ln1_g"], w["ln1_b"], (F32, BF16))

    f2 = w["w_ffn_up"].shape[1]
    tn_f = _pick(f2 // 2, (256, 128))
    if cache is None:
        act = _ffn_up(x1b, w["w_ffn_up"], w["ffn_conv_w"], w["ffn_conv_b"], None, seg=t, tm=min(tm, t), tn=tn_f)
    else:
        tm_s = _pick(m, (1024, 512)) if m >= 512 else m
        assert tm_s % t == 0 and (tm_s // t) % SUBLANES == 0 or tm_s == m
        act = _ffn_up(x1b, w["w_ffn_up"], w["ffn_conv_w"], w["ffn_conv_b"], cache["ffn"], seg=t, tm=tm_s,
                      tn=tn_f)
    pre2 = _matmul([act], w["w_ffn_down"], tm=tm2, tn=256, out_dtype=F32, resid=x1, alpha=w["alpha"])
    (y,) = _layer_norm(pre2, w["ln2_g"], w["ln2_b"], (F32,))

    x1b3 = x1b.reshape(b, t, d)
    tail = x1b3[:, t - (FFN_CONV - 1):, :].reshape(b * (FFN_CONV - 1), d)
    states = (k_new.reshape(b, t, N_KV_A, HEAD_DIM), v_new.reshape(b, t, N_KV_A, HEAD_DIM), ik_new,
              s_new, conv_state)
    return y, x1, states, tail


def _prep_weights(l, depth, w_in, conv_qkv_w, conv_qkv_b, a_log, dt_bias, delta_norm_g, rel_bias, w_o,
                  ln1_g, ln1_b, w_ffn_up, ffn_conv_w, ffn_conv_b, w_ffn_down, ln2_g, ln2_b):
    d = w_in.shape[1]
    n_heads = d // (2 * HEAD_DIM)
    w_a = n_heads * HEAD_DIM
    w_kv = N_KV_A * HEAD_DIM
    w_iq = IDX_HEADS * IDX_DIM
    c_qkv = 3 * w_a
    sizes = (w_a, w_kv, w_kv, w_iq, IDX_DIM, IDX_HEADS, c_qkv, w_a, n_heads, n_heads)
    offs = np.concatenate([[0], np.cumsum(sizes)])
    assert offs[-1] == w_in.shape[2] and n_heads == 16
    wi = w_in[l]
    seg = lambda a, b_: wi[:, offs[a]:offs[b_]]
    w_main = jnp.concatenate([seg(0, 4), seg(6, 8)], axis=1).astype(BF16)
    pad = LANES - (IDX_DIM + IDX_HEADS + 2 * n_heads)
    w_small = jnp.concatenate([seg(4, 6), seg(8, 10), jnp.zeros((d, pad), F32)], axis=1).astype(BF16)
    lane_pad = lambda v: jnp.pad(v.astype(F32), (SM_A, LANES - SM_A - n_heads)).reshape(1, LANES)
    return {
        "n_heads": n_heads,
        "alpha": float((2 * depth) ** 0.25),
        "w_main": w_main, "w_small": w_small,
        "conv_qkv_w": conv_qkv_w[l], "conv_qkv_b": conv_qkv_b[l].reshape(1, -1),
        "alog_row": lane_pad(a_log[l]), "dtb_row": lane_pad(dt_bias[l]),
        "norm_g": delta_norm_g[l].reshape(1, -1),
        "rel_bias": rel_bias,
        "w_o": w_o[l].astype(BF16),
        "ln1_g": ln1_g[l], "ln1_b": ln1_b[l],
        "w_ffn_up": w_ffn_up[l].astype(BF16),
        "ffn_conv_w": ffn_conv_w[l], "ffn_conv_b": ffn_conv_b[l].reshape(1, -1),
        "w_ffn_down": w_ffn_down[l].astype(BF16),
        "ln2_g": ln2_g[l], "ln2_b": ln2_b[l],
    }


def kernel(x_prompt, x_sample, cache_attn_k, cache_attn_v, cache_idx_k, state_delta, state_conv_qkv,
           state_ffn_conv, ln_in_g, ln_in_b, w_in, conv_qkv_w, conv_qkv_b, a_log, dt_bias, delta_norm_g,
           rel_bias, w_o, ln1_g, ln1_b, w_ffn_up, ffn_conv_w, ffn_conv_b, w_ffn_down, ln2_g, ln2_b):
    bp, tp, d = x_prompt.shape
    bs, ts, _ = x_sample.shape
    depth = w_in.shape[0]
    xp, xpb = _layer_norm(x_prompt.reshape(bp * tp, d), ln_in_g, ln_in_b, (F32, BF16))
    xs, xsb = _layer_norm(x_sample.reshape(bs * ts, d), ln_in_g, ln_in_b, (F32, BF16))
    p_states, s_states = [], []
    for l in range(depth):
        w = _prep_weights(l, depth, w_in, conv_qkv_w, conv_qkv_b, a_log, dt_bias, delta_norm_g, rel_bias,
                          w_o, ln1_g, ln1_b, w_ffn_up, ffn_conv_w, ffn_conv_b, w_ffn_down, ln2_g, ln2_b)
        cache = {"k": cache_attn_k[l], "v": cache_attn_v[l], "ik": cache_idx_k[l],
                 "delta": state_delta[l], "conv": state_conv_qkv[l], "ffn": state_ffn_conv[l]}
        xp, xp1, st_p, tail_p = _layer(xp, xpb, None, w, b=bp, t=tp)
        xs, xs1, st_s, tail_s = _layer(xs, xsb, cache, w, b=bs, t=ts)
        tail = jnp.concatenate([tail_p, tail_s], axis=0)
        ffn_state = _matmul([tail], w["w_ffn_up"], tm=tail.shape[0], tn=_pick(w["w_ffn_up"].shape[1], (512, 256, 128)),
                            out_dtype=F32)
        n_p = bp * (FFN_CONV - 1)
        p_states.append(st_p + (ffn_state[:n_p].reshape(bp, FFN_CONV - 1, -1),))
        s_states.append(st_s + (ffn_state[n_p:].reshape(bs, FFN_CONV - 1, -1),))
        if l + 1 < depth:
            xpb, xsb = xp.astype(BF16), xs.astype(BF16)
    p_out = [jnp.stack(z) for z in zip(*p_states)]
    s_out = [jnp.stack(z) for z in zip(*s_states)]
    return (xp.reshape(bp, tp, d), xs.reshape(bs, ts, d), *p_out, *s_out)
```

```python
import functools
import math

import numpy as np
import jax
import jax.numpy as jnp
from jax import lax
from jax.experimental import pallas as pl
from jax.experimental.pallas import tpu as pltpu

F32 = jnp.float32
BF16 = jnp.bfloat16

CHUNK = 64
HEAD_DIM = 128
N_KV_A = 4
IDX_HEADS = 16
IDX_DIM = 64
TOPK_MAX = 256
CONV_B = 4
FFN_CONV = 3
N_BUCKETS = 32
MAX_DIST = 128
EPS = 1e-5
IDX_SCALE = (IDX_HEADS ** -0.5) * (IDX_DIM ** -0.5)

LANES = 128
SUBLANES = 8
BF16_ROWS = 16
VMEM_LIMIT = 56 * 1024 * 1024

KEY_BLOCK = LANES
GDN_CHUNK = LANES
INV_BASE = 16
NEG = -1e30
INT_MIN = -2 ** 31

NT_DIMS = (((1,), (1,)), ((), ()))

SM_IK = 0
SM_IW = IDX_DIM
SM_BETA = SM_IW + IDX_HEADS
SM_A = SM_BETA + 16


def _params(n_grid_axes):
    return pltpu.CompilerParams(dimension_semantics=("arbitrary",) * n_grid_axes,
                                vmem_limit_bytes=VMEM_LIMIT)


def _dot(a, b, dims=None):
    a = a.astype(BF16)
    b = b.astype(BF16)
    if dims is None:
        return jnp.dot(a, b, preferred_element_type=F32)
    return lax.dot_general(a, b, dims, preferred_element_type=F32)


def _dot_exact(a, b):
    return jnp.dot(a, b, preferred_element_type=F32, precision=lax.Precision.HIGHEST)


def _silu(x):
    return x * (1.0 / (1.0 + jnp.exp(-x)))


def _ln_kernel(x_ref, g_ref, b_ref, *out_refs):
    x = x_ref[...]
    mu = jnp.mean(x, axis=-1, keepdims=True)
    xc = x - mu
    var = jnp.mean(xc * xc, axis=-1, keepdims=True)
    y = xc * lax.rsqrt(var + EPS) * g_ref[...] + b_ref[...]
    for o_ref in out_refs:
        o_ref[...] = y.astype(o_ref.dtype)


def _layer_norm(x, g, b, out_dtypes):
    m, d = x.shape
    tr = _pick(m, (256, 128, 64))
    row = pl.BlockSpec((tr, d), lambda i: (i, 0))
    vec = pl.BlockSpec((1, d), lambda i: (0, 0))
    outs = pl.pallas_call(
        _ln_kernel,
        out_shape=[jax.ShapeDtypeStruct((m, d), dt) for dt in out_dtypes],
        grid=(m // tr,),
        in_specs=[row, vec, vec],
        out_specs=[row for _ in out_dtypes],
        compiler_params=_params(1),
        name="layer_norm",
    )(x, g.reshape(1, d), b.reshape(1, d))
    return outs


def _mm_kernel(*refs, k_sizes, alpha, has_resid):
    n_lhs = len(k_sizes)
    w_ref = refs[n_lhs]
    o_ref = refs[-1]
    acc = None
    off = 0
    for a_ref, kp in zip(refs[:n_lhs], k_sizes):
        d = jnp.dot(a_ref[...], w_ref[off:off + kp, :], preferred_element_type=F32)
        acc = d if acc is None else acc + d
        off += kp
    if has_resid:
        acc = acc + alpha * refs[n_lhs + 1][...]
    o_ref[...] = acc.astype(o_ref.dtype)


def _matmul(lhs_list, w, *, tm, tn, out_dtype, resid=None, alpha=1.0):
    m = lhs_list[0].shape[0]
    k_sizes = tuple(a.shape[1] for a in lhs_list)
    k, n = w.shape
    assert sum(k_sizes) == k and m % tm == 0 and n % tn == 0
    in_specs = [pl.BlockSpec((tm, kp), lambda i, j: (i, 0)) for kp in k_sizes]
    in_specs.append(pl.BlockSpec((k, tn), lambda i, j: (0, j)))
    args = list(lhs_list) + [w]
    if resid is not None:
        in_specs.append(pl.BlockSpec((tm, tn), lambda i, j: (i, j)))
        args.append(resid)
    return pl.pallas_call(
        functools.partial(_mm_kernel, k_sizes=k_sizes, alpha=alpha, has_resid=resid is not None),
        out_shape=jax.ShapeDtypeStruct((m, n), out_dtype),
        grid=(m // tm, n // tn),
        in_specs=in_specs,
        out_specs=pl.BlockSpec((tm, tn), lambda i, j: (i, j)),
        compiler_params=_params(2),
        name="matmul",
    )(*args)


def _t5_bucket_np(rel):
    half = N_BUCKETS // 2
    max_exact = half // 2
    n = np.abs(rel)
    large = max_exact + np.floor(2.0 * np.log2(np.maximum(n, 1) / max_exact) + 1e-9).astype(np.int64)
    large = np.minimum(large, half - 1)
    return np.where(rel > 0, half, 0) + np.where(n < max_exact, n, large)


def _bias_tiles(rel_bias, tq):
    n_heads = rel_bias.shape[1]
    group = n_heads // N_KV_A
    i = np.arange(tq)[:, None]
    j = np.arange(KEY_BLOCK)[None, :]
    tiles = []
    for r in range(3):
        rel = -r * KEY_BLOCK + j - i
        bucket = _t5_bucket_np(rel)
        if r == 2:
            assert (bucket == N_BUCKETS // 2 - 1).all()
        t = rel_bias[bucket.reshape(-1)].reshape(tq, KEY_BLOCK, N_KV_A, group)
        tiles.append(jnp.transpose(t, (2, 3, 0, 1)).reshape(N_KV_A, group * tq, KEY_BLOCK))
    return jnp.stack(tiles).astype(F32)


def _sortable(x):
    i = lax.bitcast_convert_type(x, jnp.int32)
    return jnp.where(i >= 0, i, i ^ jnp.int32(0x7FFFFFFF))


def _dsa_kernel(q_ref, iq_ref, sm_ref, k_ref, v_ref, ik_ref, bias_ref, o_ref,
                key_scr, mask_scr, logit_scr, m_scr, l_scr, acc_scr, *, tq, qb0, top_k, group):
    qb = pl.program_id(1) + qb0
    nkb = qb + 1
    rows = group * tq

    row = lax.broadcasted_iota(jnp.int32, (tq, KEY_BLOCK), 0)
    lane = lax.broadcasted_iota(jnp.int32, (tq, KEY_BLOCK), 1)
    limit = ((qb * KEY_BLOCK + row) // CHUNK + 1) * CHUNK
    lane_k = lax.broadcasted_iota(jnp.int32, (KEY_BLOCK, LANES), 1)

    iw = sm_ref[:, SM_IW:SM_IW + IDX_HEADS] * IDX_SCALE
    iqb = iq_ref[...].astype(BF16)
    iw_b = [jnp.broadcast_to(iw[:, h:h + 1], (tq, KEY_BLOCK)) for h in range(IDX_HEADS)]

    def idx_body(r, carry):
        kb = qb - r
        ik = ik_ref[pl.ds(pl.multiple_of(kb * KEY_BLOCK, KEY_BLOCK), KEY_BLOCK), :]
        ik_lo = jnp.where(lane_k < IDX_DIM, ik, 0.0)
        ik_hi = pltpu.roll(ik_lo, IDX_DIM, axis=1)
        rhs = jnp.concatenate([ik_lo, ik_hi], axis=0).astype(BF16)
        acc = jnp.zeros((tq, KEY_BLOCK), F32)
        for g in range(IDX_HEADS // 2):
            s = lax.dot_general(iqb[:, g * LANES:(g + 1) * LANES], rhs, NT_DIMS,
                                preferred_element_type=F32)
            acc = acc + jnp.maximum(s[:, :KEY_BLOCK], 0.0) * iw_b[2 * g]
            acc = acc + jnp.maximum(s[:, KEY_BLOCK:], 0.0) * iw_b[2 * g + 1]
        admissible = kb * KEY_BLOCK + lane < limit
        key_scr[r] = jnp.where(admissible, _sortable(acc), INT_MIN)
        return carry

    lax.fori_loop(0, nkb, idx_body, 0)

    def count_ge(cand):
        cand_b = jnp.broadcast_to(cand, (tq, KEY_BLOCK))

        def body(r, c):
            return c + jnp.where(key_scr[r] >= cand_b, 1.0, 0.0)

        c = lax.fori_loop(0, nkb, body, jnp.zeros((tq, KEY_BLOCK), F32))
        return jnp.sum(c, axis=1, keepdims=True)

    kf = float(top_k)
    zero = jnp.zeros((tq, 1), jnp.int32)
    thr = jnp.where(count_ge(zero) >= kf, zero, jnp.int32(INT_MIN))

    def bit_body(it, thr):
        cand = thr | jnp.left_shift(jnp.int32(1), 30 - it)
        return jnp.where(count_ge(cand) >= kf, cand, thr)

    thr = lax.fori_loop(0, 31, bit_body, thr)
    thr_b = jnp.broadcast_to(thr, (tq, KEY_BLOCK))

    def mask_body(r, carry):
        key = key_scr[r]
        sel = (key >= thr_b) & (key != INT_MIN)
        m = jnp.where(sel, 0.0, NEG)
        mask_scr[r] = jnp.concatenate([m] * group, axis=0)
        return carry

    lax.fori_loop(0, nkb, mask_body, 0)

    scale = HEAD_DIM ** -0.5
    for n in range(N_KV_A):
        q4 = jnp.concatenate(
            [q_ref[:, (group * n + g) * HEAD_DIM:(group * n + g + 1) * HEAD_DIM] for g in range(group)],
            axis=0).astype(BF16)
        m_scr[...] = jnp.full((rows, KEY_BLOCK), NEG, F32)

        def kv_block(ref, r):
            kb = qb - r
            return ref[pl.ds(pl.multiple_of(kb * KEY_BLOCK, KEY_BLOCK), KEY_BLOCK),
                       n * HEAD_DIM:(n + 1) * HEAD_DIM].astype(BF16)

        def logits_body(it, carry):
            rs = [jnp.minimum(2 * it + u, nkb - 1) for u in range(2)]
            ks = [kv_block(k_ref, r) for r in rs]
            ss = [lax.dot_general(q4, kblk, NT_DIMS, preferred_element_type=F32) for kblk in ks]
            ss = [s * scale + bias_ref[jnp.minimum(r, 2), n] + mask_scr[r] for s, r in zip(ss, rs)]
            for s, r in zip(ss, rs):
                logit_scr[r] = s
            m_scr[...] = jnp.maximum(m_scr[...], jnp.maximum(ss[0], ss[1]))
            return carry

        lax.fori_loop(0, (nkb + 1) // 2, logits_body, 0)
        m = jnp.max(m_scr[...], axis=1, keepdims=True)
        l_scr[...] = jnp.zeros((rows, KEY_BLOCK), F32)
        acc_scr[...] = jnp.zeros((rows, HEAD_DIM), F32)

        def pv_body(it, carry):
            rs = [jnp.minimum(2 * it + u, nkb - 1) for u in range(2)]
            live = [1.0, jnp.where(2 * it + 1 < nkb, 1.0, 0.0)]
            ps = [jnp.exp(logit_scr[r] - m) * w for r, w in zip(rs, live)]
            vs = [kv_block(v_ref, r) for r in rs]
            pv = [jnp.dot(p.astype(BF16), vblk, preferred_element_type=F32) for p, vblk in zip(ps, vs)]
            l_scr[...] += ps[0] + ps[1]
            acc_scr[...] += pv[0] + pv[1]
            return carry

        lax.fori_loop(0, (nkb + 1) // 2, pv_body, 0)
        out = acc_scr[...] / jnp.sum(l_scr[...], axis=1, keepdims=True)
        for g in range(group):
            h = group * n + g
            o_ref[:, h * HEAD_DIM:(h + 1) * HEAD_DIM] = out[g * tq:(g + 1) * tq].astype(o_ref.dtype)


def _dsa(q_src, q_blk, iq_src, iq_blk, small, k_src, k_blk, v_src, v_blk, ik_src, bias, *,
         t, tq, qb0, top_k, n_heads):
    b = q_src.shape[0]
    l = k_src.shape[1]
    group = n_heads // N_KV_A
    nq = t // tq
    nkb_max = qb0 + nq
    assert l == nkb_max * KEY_BLOCK and t % tq == 0 and tq % CHUNK == 0 and tq <= KEY_BLOCK
    assert tq == KEY_BLOCK or nq == 1
    w_a = n_heads * HEAD_DIM
    w_kv = N_KV_A * HEAD_DIM
    w_iq = IDX_HEADS * IDX_DIM
    rows = group * tq
    kern = functools.partial(_dsa_kernel, tq=tq, qb0=qb0, top_k=top_k, group=group)
    return pl.pallas_call(
        kern,
        out_shape=jax.ShapeDtypeStruct((b, t, w_a), BF16),
        grid=(b, nq),
        in_specs=[
            pl.BlockSpec((None, tq, w_a), lambda bi, i: (bi, i, q_blk)),
            pl.BlockSpec((None, tq, w_iq), lambda bi, i: (bi, i, iq_blk)),
            pl.BlockSpec((None, tq, LANES), lambda bi, i: (bi, i, 0)),
            pl.BlockSpec((None, l, w_kv), lambda bi, i: (bi, 0, k_blk)),
            pl.BlockSpec((None, l, w_kv), lambda bi, i: (bi, 0, v_blk)),
            pl.BlockSpec((None, l, LANES), lambda bi, i: (bi, 0, 0)),
            pl.BlockSpec((3, N_KV_A, rows, KEY_BLOCK), lambda bi, i: (0, 0, 0, 0)),
        ],
        out_specs=pl.BlockSpec((None, tq, w_a), lambda bi, i: (bi, i, 0)),
        scratch_shapes=[
            pltpu.VMEM((nkb_max, tq, KEY_BLOCK), jnp.int32),
            pltpu.VMEM((nkb_max, rows, KEY_BLOCK), F32),
            pltpu.VMEM((nkb_max, rows, KEY_BLOCK), F32),
            pltpu.VMEM((rows, KEY_BLOCK), F32),
            pltpu.VMEM((rows, KEY_BLOCK), F32),
            pltpu.VMEM((rows, HEAD_DIM), F32),
        ],
        compiler_params=_params(2),
        name="dsa",
    )(q_src, iq_src, small, k_src, v_src, ik_src, bias)


def _gdn_kernel(xq_ref, xk_ref, xv_ref, z_ref, sm_ref, pq_ref, pk_ref, pv_ref,
                wq_ref, wk_ref, wv_ref, bq_ref, bk_ref, bv_ref, alog_ref, dtb_ref, ng_ref, s0_ref,
                o_ref, s_ref, gc_scr, gt_scr, u_scr, w_scr, qd_scr, kdt_scr, qk_scr, gl_scr, *, t, hb, c):
    hg = pl.program_id(1)
    n_chunks = t // c

    ri = lax.broadcasted_iota(jnp.int32, (c, c), 0)
    ci = lax.broadcasted_iota(jnp.int32, (c, c), 1)
    tri = ri >= ci
    strict = ri > ci
    tril = jnp.where(tri, 1.0, 0.0)
    eye = jnp.where(ri == ci, 1.0, 0.0)
    level_masks = [ri // INV_BASE == ci // INV_BASE]
    size = INV_BASE
    while size < c:
        level_masks.append((ri // (2 * size) == ci // (2 * size)) & (ri // size != ci // size))
        size *= 2
    a_scale = -jnp.exp(alog_ref[...])
    dtb = dtb_ref[...]
    for bi in range(n_chunks):
        a = sm_ref[bi * c:(bi + 1) * c, :] + dtb
        g = a_scale * (jnp.maximum(a, 0.0) + jnp.log1p(jnp.exp(-jnp.abs(a))))
        gc = _dot_exact(tril, g)
        gc_scr[bi * c:(bi + 1) * c, :] = gc
        if c < LANES:
            gc = jnp.concatenate([gc, jnp.zeros((LANES - c, LANES), F32)], axis=0)
        gt_scr[bi] = gc.T[:, :c]

    lane = lax.broadcasted_iota(jnp.int32, (c, LANES), 1)
    ng = ng_ref[...]

    def conv(win, w_ref, b_ref, hh):
        cols = slice(hh * HEAD_DIM, (hh + 1) * HEAD_DIM)
        y = b_ref[:, cols]
        for j in range(CONV_B):
            lo = SUBLANES - (CONV_B - 1) + j
            y = y + win[lo:lo + c, cols] * w_ref[j:j + 1, cols]
        return _silu(y)

    def l2n(x):
        return x * lax.rsqrt(jnp.sum(x * x, axis=-1, keepdims=True) + 1e-6)

    def window(x_ref, p_ref, ck, row0):
        halo = x_ref[pl.ds(pl.multiple_of(jnp.maximum(row0 - SUBLANES, 0), SUBLANES), SUBLANES), :]
        halo = jnp.where(ck == 0, p_ref[...], halo)
        return jnp.concatenate([halo, x_ref[pl.ds(row0, c), :]], axis=0)

    cpi = 2 if n_chunks % 2 == 0 else 1

    def prep(it, carry):
        chains = []
        for sub in range(cpi):
            ck = it * cpi + sub
            row0 = pl.multiple_of(ck * c, c)
            wq = window(xq_ref, pq_ref, ck, row0)
            wk = window(xk_ref, pk_ref, ck, row0)
            wv = window(xv_ref, pv_ref, ck, row0)
            beta_all = 1.0 / (1.0 + jnp.exp(-sm_ref[pl.ds(row0, c), :]))
            gc_all = gc_scr[pl.ds(row0, c), :]
            for hh in range(hb):
                h = hg * hb + hh
                ch = {"ck": ck, "row0": row0, "hh": hh}
                ch["beta"] = jnp.sum(jnp.where(lane == SM_BETA + h, beta_all, 0.0), axis=1, keepdims=True)
                gcol = jnp.sum(jnp.where(lane == SM_A + h, gc_all, 0.0), axis=1, keepdims=True)
                grow = gt_scr[ck, pl.ds(SM_A + h, 1), :]
                ch["glast"] = gcol[c - 1:c, :]
                ch["gcol"] = gcol
                ch["dmask"] = jnp.where(tri, jnp.exp(jnp.where(tri, gcol - grow, 0.0)), 0.0)
                ch["q"] = l2n(conv(wq, wq_ref, bq_ref, hh)) * (HEAD_DIM ** -0.5)
                ch["k"] = l2n(conv(wk, wk_ref, bk_ref, hh))
                ch["v"] = conv(wv, wv_ref, bv_ref, hh)
                ch["kb"] = ch["k"] * ch["beta"]
                chains.append(ch)
        for ch in chains:
            ch["kk"] = _dot(ch["kb"], ch["k"], NT_DIMS)
            ch["qk"] = _dot(ch["q"], ch["k"], NT_DIMS)
        for ch in chains:
            ch["lmat"] = jnp.where(strict, ch["kk"] * ch["dmask"], 0.0)
            ch["npow"] = jnp.where(level_masks[0], -ch["lmat"], 0.0)
            ch["ainv"] = eye + ch["npow"]
        for _ in range(int(math.log2(INV_BASE)) - 1):
            for ch in chains:
                ch["npow"] = _dot(ch["npow"], ch["npow"])
            for ch in chains:
                ch["ainv"] = ch["ainv"] + _dot(ch["ainv"], ch["npow"])
        for lm in level_masks[1:]:
            for ch in chains:
                ch["tmp"] = _dot(ch["ainv"], jnp.where(lm, ch["lmat"], 0.0))
            for ch in chains:
                ch["ainv"] = ch["ainv"] - _dot(ch["tmp"], ch["ainv"])
        for ch in chains:
            ch["eg"] = jnp.exp(ch["gcol"])
            ch["sol"] = _dot(ch["ainv"], jnp.concatenate([ch["v"] * ch["beta"], ch["kb"] * ch["eg"]], axis=1))
        for ch in chains:
            ck, row0, hh = ch["ck"], ch["row0"], ch["hh"]
            qk = jnp.where(tri, ch["qk"] * ch["dmask"], 0.0)
            k_dec = ch["k"] * jnp.exp(ch["glast"] - ch["gcol"])
            u_scr[hh, pl.ds(row0, c), :] = ch["sol"][:, :HEAD_DIM]
            w_scr[hh, pl.ds(row0, c), :] = ch["sol"][:, HEAD_DIM:].astype(BF16)
            qd_scr[hh, pl.ds(row0, c), :] = (ch["q"] * ch["eg"]).astype(BF16)
            kdt_scr[hh, ck] = k_dec.T.astype(BF16) if c == LANES else jnp.concatenate(
                [k_dec, jnp.zeros((LANES - c, HEAD_DIM), F32)], axis=0).T[:, :c].astype(BF16)
            qk_scr[hh, ck] = qk.astype(BF16)
            gl_scr[hh, pl.ds(ck, 1), :] = jnp.broadcast_to(jnp.exp(ch["glast"]), (1, LANES))
        return carry

    lax.fori_loop(0, n_chunks // cpi, prep, 0)

    def scan(ck, states):
        row0 = pl.multiple_of(ck * c, c)
        new_states = []
        for hh in range(hb):
            cols = slice(hh * HEAD_DIM, (hh + 1) * HEAD_DIM)
            s = states[hh]
            sb = s.astype(BF16)
            v_new = u_scr[hh, pl.ds(row0, c), :] - jnp.dot(w_scr[hh, pl.ds(row0, c), :], sb,
                                                         preferred_element_type=F32)
            vb = v_new.astype(BF16)
            o = jnp.dot(qd_scr[hh, pl.ds(row0, c), :], sb, preferred_element_type=F32)
            o = o + jnp.dot(qk_scr[hh, ck], vb, preferred_element_type=F32)
            s = s * gl_scr[hh, pl.ds(ck, 1), :] + jnp.dot(kdt_scr[hh, ck], vb, preferred_element_type=F32)
            new_states.append(s)
            o = o * lax.rsqrt(jnp.mean(o * o, axis=-1, keepdims=True) + EPS) * ng
            o = o * _silu(z_ref[pl.ds(row0, c), cols])
            o_ref[pl.ds(row0, c), cols] = o.astype(o_ref.dtype)
        return tuple(new_states)

    states = lax.fori_loop(0, n_chunks, scan, tuple(s0_ref[hh] for hh in range(hb)))
    for hh in range(hb):
        s_ref[hh] = states[hh]


def _gdn(h_main, qkv_col0, z_col0, small, prev8, conv_w, conv_b, alog_row, dtb_row, norm_g, s0, *, hb):
    b, t, _ = h_main.shape
    n_heads = s0.shape[1]
    w_b = n_heads * HEAD_DIM
    hw = hb * HEAD_DIM
    ng = n_heads // hb
    c = GDN_CHUNK if t % GDN_CHUNK == 0 else CHUNK
    n_chunks = t // c
    assert t % c == 0 and n_heads % hb == 0

    def col(c0):
        return lambda bi, g: (bi, 0, c0 // hw + g)

    def wcol(c0):
        return lambda bi, g: (0, c0 // hw + g)

    tok = lambda c0: pl.BlockSpec((None, t, hw), col(c0))
    prev = lambda c0: pl.BlockSpec((None, SUBLANES, hw), col(c0))
    cw = lambda c0: pl.BlockSpec((CONV_B, hw), wcol(c0))
    cb = lambda c0: pl.BlockSpec((1, hw), wcol(c0))
    vec = pl.BlockSpec((1, LANES), lambda bi, g: (0, 0))
    state = pl.BlockSpec((None, hb, HEAD_DIM, HEAD_DIM), lambda bi, g: (bi, g, 0, 0))
    return pl.pallas_call(
        functools.partial(_gdn_kernel, t=t, hb=hb, c=c),
        out_shape=[jax.ShapeDtypeStruct((b, t, w_b), BF16),
                   jax.ShapeDtypeStruct(s0.shape, F32)],
        grid=(b, ng),
        in_specs=[tok(qkv_col0), tok(qkv_col0 + w_b), tok(qkv_col0 + 2 * w_b), tok(z_col0),
                  pl.BlockSpec((None, t, LANES), lambda bi, g: (bi, 0, 0)),
                  prev(0), prev(w_b), prev(2 * w_b),
                  cw(0), cw(w_b), cw(2 * w_b), cb(0), cb(w_b), cb(2 * w_b),
                  vec, vec, vec, state],
        out_specs=[pl.BlockSpec((None, t, hw), lambda bi, g: (bi, 0, g)), state],
        scratch_shapes=[pltpu.VMEM((t, LANES), F32),
                        pltpu.VMEM((n_chunks, LANES, c), F32),
                        pltpu.VMEM((hb, t, HEAD_DIM), F32),
                        pltpu.VMEM((hb, t, HEAD_DIM), BF16),
                        pltpu.VMEM((hb, t, HEAD_DIM), BF16),
                        pltpu.VMEM((hb, n_chunks, HEAD_DIM, c), BF16),
                        pltpu.VMEM((hb, n_chunks, c, c), BF16),
                        pltpu.VMEM((hb, max(n_chunks, SUBLANES), LANES), F32)],
        compiler_params=_params(2),
        name="gdn",
    )(h_main, h_main, h_main, h_main, small, prev8, prev8, prev8,
      conv_w, conv_w, conv_w, conv_b, conv_b, conv_b, alog_row, dtb_row, norm_g, s0)


def _ffn_up_kernel(*refs, tm, seg, has_prev):
    x_ref, wg_ref, wv_ref, cwg_ref, cwv_ref, cbg_ref, cbv_ref = refs[:7]
    n_in = 11 if has_prev else 7
    o_ref = refs[n_in]
    ext_scrs = refs[n_in + 1:n_in + 3]
    carry_scr = refs[n_in + 3]
    i = pl.program_id(0)
    j = pl.program_id(1)
    tn = o_ref.shape[1]
    r = (i * tm + lax.broadcasted_iota(jnp.int32, (tm, tn), 0)) % seg

    @pl.when(i == 0)
    def _():
        carry_scr[0, j] = jnp.zeros((SUBLANES, tn), F32)
        carry_scr[1, j] = jnp.zeros((SUBLANES, tn), F32)

    if has_prev:
        nseg = tm // seg
        srow = lax.broadcasted_iota(jnp.int32, (tm, nseg), 0)
        scol = lax.broadcasted_iota(jnp.int32, (tm, nseg), 1) * seg
        place0 = jnp.where(srow == scol, 1.0, 0.0)
        place1 = jnp.where(srow == scol + 1, 1.0, 0.0)

    def branch(slot, w_ref, cw_ref, cb_ref, prev_refs):
        ext_scr = ext_scrs[slot]
        up = jnp.dot(x_ref[...], w_ref[...], preferred_element_type=F32)
        ext_scr[0:SUBLANES, :] = carry_scr[slot, j]
        ext_scr[SUBLANES:, :] = up
        carry_scr[slot, j] = up[tm - SUBLANES:, :]
        y = cb_ref[...] + up * cw_ref[2:3, :]
        y = y + jnp.where(r >= 1, ext_scr[SUBLANES - 1:SUBLANES - 1 + tm, :], 0.0) * cw_ref[1:2, :]
        y = y + jnp.where(r >= 2, ext_scr[SUBLANES - 2:SUBLANES - 2 + tm, :], 0.0) * cw_ref[0:1, :]
        if has_prev:
            p0, p1 = prev_refs[0][...], prev_refs[1][...]
            y = y + _dot_exact(place0, p0 * cw_ref[0:1, :] + p1 * cw_ref[1:2, :])
            y = y + _dot_exact(place1, p1 * cw_ref[0:1, :])
        return y

    gate = branch(0, wg_ref, cwg_ref, cbg_ref, refs[7:9])
    val = branch(1, wv_ref, cwv_ref, cbv_ref, refs[9:11])
    o_ref[...] = (_silu(gate) * val).astype(o_ref.dtype)


def _ffn_up(x1b, w_up, conv_w, conv_b, prev, *, seg, tm, tn):
    m, d = x1b.shape
    f = w_up.shape[1] // 2
    nj = f // tn
    has_prev = prev is not None
    assert m % tm == 0 and f % tn == 0 and (seg % tm == 0 or tm % seg == 0) and tm % BF16_ROWS == 0
    in_specs = [
        pl.BlockSpec((tm, d), lambda i, j: (i, 0)),
        pl.BlockSpec((d, tn), lambda i, j: (0, j)),
        pl.BlockSpec((d, tn), lambda i, j: (0, nj + j)),
        pl.BlockSpec((FFN_CONV, tn), lambda i, j: (0, j)),
        pl.BlockSpec((FFN_CONV, tn), lambda i, j: (0, nj + j)),
        pl.BlockSpec((1, tn), lambda i, j: (0, j)),
        pl.BlockSpec((1, tn), lambda i, j: (0, nj + j)),
    ]
    args = [x1b, w_up, w_up, conv_w, conv_w, conv_b, conv_b]
    if has_prev:
        nseg = tm // seg
        p0, p1 = prev[:, 0, :], prev[:, 1, :]
        in_specs += [pl.BlockSpec((nseg, tn), lambda i, j: (i, j)),
                     pl.BlockSpec((nseg, tn), lambda i, j: (i, j)),
                     pl.BlockSpec((nseg, tn), lambda i, j: (i, nj + j)),
                     pl.BlockSpec((nseg, tn), lambda i, j: (i, nj + j))]
        args += [p0, p1, p0, p1]
    return pl.pallas_call(
        functools.partial(_ffn_up_kernel, tm=tm, seg=seg, has_prev=has_prev),
        out_shape=jax.ShapeDtypeStruct((m, f), BF16),
        grid=(m // tm, nj),
        in_specs=in_specs,
        out_specs=pl.BlockSpec((tm, tn), lambda i, j: (i, j)),
        scratch_shapes=[pltpu.VMEM((SUBLANES + tm, tn), F32),
                        pltpu.VMEM((SUBLANES + tm, tn), F32),
                        pltpu.VMEM((2, nj, SUBLANES, tn), F32)],
        compiler_params=_params(2),
        name="ffn_up",
    )(*args)


def _pick(n, candidates):
    for c in candidates:
        if n % c == 0:
            return c
    raise ValueError(f"no tile for {n}")


def _layer(xn, xnb, cache, w, *, b, t):
    m, d = xn.shape
    n_heads = w["n_heads"]
    w_a = n_heads * HEAD_DIM
    w_kv = N_KV_A * HEAD_DIM
    w_iq = IDX_HEADS * IDX_DIM
    c_qkv = 3 * w_a
    tm = _pick(m, (1024, 512, 256, 128, 64))

    h_main = _matmul([xnb], w["w_main"], tm=tm, tn=512, out_dtype=F32)
    small = _matmul([xnb], w["w_small"], tm=tm, tn=LANES, out_dtype=F32)
    h3 = h_main.reshape(b, t, -1)
    small3 = small.reshape(b, t, LANES)
    col_k, col_v, col_iq, col_qkv, col_z = w_a, w_a + w_kv, w_a + 2 * w_kv, w_a + 2 * w_kv + w_iq, None
    col_z = col_qkv + c_qkv
    k_new = h3[:, :, col_k:col_k + w_kv]
    v_new = h3[:, :, col_v:col_v + w_kv]
    ik_new = small3[:, :, SM_IK:SM_IK + IDX_DIM]
    qkv_b = h3[:, :, col_qkv:col_qkv + c_qkv]

    if cache is None:
        past = 0
        tq = KEY_BLOCK
        assert t % KEY_BLOCK == 0
        k_src, k_blk, v_src, v_blk, ik_src = h3, col_k // w_kv, h3, col_v // w_kv, small3
    else:
        past = cache["k"].shape[1]
        tq = t
        assert past % KEY_BLOCK == 0 and t <= KEY_BLOCK
        pad = KEY_BLOCK - t
        k_src = jnp.concatenate([cache["k"].reshape(b, past, w_kv), k_new,
                                 jnp.zeros((b, pad, w_kv), F32)], axis=1)
        v_src = jnp.concatenate([cache["v"].reshape(b, past, w_kv), v_new,
                                 jnp.zeros((b, pad, w_kv), F32)], axis=1)
        ik_all = jnp.concatenate([cache["ik"], ik_new, jnp.zeros((b, pad, IDX_DIM), F32)], axis=1)
        ik_src = jnp.pad(ik_all, ((0, 0), (0, 0), (0, LANES - IDX_DIM)))
        k_blk = v_blk = 0
    top_k = min(TOPK_MAX, (past + t) // 4)
    out_a = _dsa(h3, 0, h3, col_iq // w_iq, small3, k_src, k_blk, v_src, v_blk, ik_src,
                 _bias_tiles(w["rel_bias"], tq), t=t, tq=tq, qb0=past // KEY_BLOCK, top_k=top_k,
                 n_heads=n_heads)

    if cache is None:
        prev8 = jnp.zeros((b, SUBLANES, c_qkv), F32)
        s0 = jnp.zeros((b, n_heads, HEAD_DIM, HEAD_DIM), F32)
    else:
        prev8 = jnp.pad(cache["conv"], ((0, 0), (SUBLANES - (CONV_B - 1), 0), (0, 0)))
        s0 = cache["delta"]
    o_b, s_new = _gdn(h3, col_qkv, col_z, small3, prev8, w["conv_qkv_w"], w["conv_qkv_b"],
                      w["alog_row"], w["dtb_row"], w["norm_g"], s0, hb=2 if cache is None else 8)
    if t >= CONV_B - 1:
        conv_state = qkv_b[:, t - (CONV_B - 1):, :]
    else:
        conv_state = jnp.concatenate([cache["conv"], qkv_b], axis=1)[:, -(CONV_B - 1):, :]

    tm2 = _pick(m, (512, 256, 128, 64))
    pre1 = _matmul([out_a.reshape(m, w_a), o_b.reshape(m, w_a)], w["w_o"], tm=tm2, tn=512,
                   out_dtype=F32, resid=xn, alpha=w["alpha"])
    x1, x1b = _layer_norm(pre1, w["ln1_g"], w["ln1_b"], (F32, BF16))

    f2 = w["w_ffn_up"].shape[1]
    tn_f = _pick(f2 // 2, (256, 128))
    if cache is None:
        act = _ffn_up(x1b, w["w_ffn_up"], w["ffn_conv_w"], w["ffn_conv_b"], None, seg=t, tm=min(tm, t), tn=tn_f)
    else:
        tm_s = _pick(m, (1024, 512)) if m >= 512 else m
        assert tm_s % t == 0 and (tm_s // t) % SUBLANES == 0 or tm_s == m
        act = _ffn_up(x1b, w["w_ffn_up"], w["ffn_conv_w"], w["ffn_conv_b"], cache["ffn"], seg=t, tm=tm_s,
                      tn=tn_f)
    pre2 = _matmul([act], w["w_ffn_down"], tm=tm2, tn=256, out_dtype=F32, resid=x1, alpha=w["alpha"])
    (y,) = _layer_norm(pre2, w["ln2_g"], w["ln2_b"], (F32,))

    x1b3 = x1b.reshape(b, t, d)
    tail = x1b3[:, t - (FFN_CONV - 1):, :].reshape(b * (FFN_CONV - 1), d)
    states = (k_new.reshape(b, t, N_KV_A, HEAD_DIM), v_new.reshape(b, t, N_KV_A, HEAD_DIM), ik_new,
              s_new, conv_state)
    return y, x1, states, tail


def _prep_weights(l, depth, w_in, conv_qkv_w, conv_qkv_b, a_log, dt_bias, delta_norm_g, rel_bias, w_o,
                  ln1_g, ln1_b, w_ffn_up, ffn_conv_w, ffn_conv_b, w_ffn_down, ln2_g, ln2_b):
    d = w_in.shape[1]
    n_heads = d // (2 * HEAD_DIM)
    w_a = n_heads * HEAD_DIM
    w_kv = N_KV_A * HEAD_DIM
    w_iq = IDX_HEADS * IDX_DIM
    c_qkv = 3 * w_a
    sizes = (w_a, w_kv, w_kv, w_iq, IDX_DIM, IDX_HEADS, c_qkv, w_a, n_heads, n_heads)
    offs = np.concatenate([[0], np.cumsum(sizes)])
    assert offs[-1] == w_in.shape[2] and n_heads == 16
    wi = w_in[l]
    seg = lambda a, b_: wi[:, offs[a]:offs[b_]]
    w_main = jnp.concatenate([seg(0, 4), seg(6, 8)], axis=1).astype(BF16)
    pad = LANES - (IDX_DIM + IDX_HEADS + 2 * n_heads)
    w_small = jnp.concatenate([seg(4, 6), seg(8, 10), jnp.zeros((d, pad), F32)], axis=1).astype(BF16)
    lane_pad = lambda v: jnp.pad(v.astype(F32), (SM_A, LANES - SM_A - n_heads)).reshape(1, LANES)
    return {
        "n_heads": n_heads,
        "alpha": float((2 * depth) ** 0.25),
        "w_main": w_main, "w_small": w_small,
        "conv_qkv_w": conv_qkv_w[l], "conv_qkv_b": conv_qkv_b[l].reshape(1, -1),
        "alog_row": lane_pad(a_log[l]), "dtb_row": lane_pad(dt_bias[l]),
        "norm_g": delta_norm_g[l].reshape(1, -1),
        "rel_bias": rel_bias,
        "w_o": w_o[l].astype(BF16),
        "ln1_g": ln1_g[l], "ln1_b": ln1_b[l],
        "w_ffn_up": w_ffn_up[l].astype(BF16),
        "ffn_conv_w": ffn_conv_w[l], "ffn_conv_b": ffn_conv_b[l].reshape(1, -1),
        "w_ffn_down": w_ffn_down[l].astype(BF16),
        "ln2_g": ln2_g[l], "ln2_b": ln2_b[l],
    }


def kernel(x_prompt, x_sample, cache_attn_k, cache_attn_v, cache_idx_k, state_delta, state_conv_qkv,
           state_ffn_conv, ln_in_g, ln_in_b, w_in, conv_qkv_w, conv_qkv_b, a_log, dt_bias, delta_norm_g,
           rel_bias, w_o, ln1_g, ln1_b, w_ffn_up, ffn_conv_w, ffn_conv_b, w_ffn_down, ln2_g, ln2_b):
    bp, tp, d = x_prompt.shape
    bs, ts, _ = x_sample.shape
    depth = w_in.shape[0]
    xp, xpb = _layer_norm(x_prompt.reshape(bp * tp, d), ln_in_g, ln_in_b, (F32, BF16))
    xs, xsb = _layer_norm(x_sample.reshape(bs * ts, d), ln_in_g, ln_in_b, (F32, BF16))
    p_states, s_states = [], []
    for l in range(depth):
        w = _prep_weights(l, depth, w_in, conv_qkv_w, conv_qkv_b, a_log, dt_bias, delta_norm_g, rel_bias,
                          w_o, ln1_g, ln1_b, w_ffn_up, ffn_conv_w, ffn_conv_b, w_ffn_down, ln2_g, ln2_b)
        cache = {"k": cache_attn_k[l], "v": cache_attn_v[l], "ik": cache_idx_k[l],
                 "delta": state_delta[l], "conv": state_conv_qkv[l], "ffn": state_ffn_conv[l]}
        xp, xp1, st_p, tail_p = _layer(xp, xpb, None, w, b=bp, t=tp)
        xs, xs1, st_s, tail_s = _layer(xs, xsb, cache, w, b=bs, t=ts)
        tail = jnp.concatenate([tail_p, tail_s], axis=0)
        ffn_state = _matmul([tail], w["w_ffn_up"], tm=tail.shape[0], tn=_pick(w["w_ffn_up"].shape[1], (512, 256, 128)),
                            out_dtype=F32)
        n_p = bp * (FFN_CONV - 1)
        p_states.append(st_p + (ffn_state[:n_p].reshape(bp, FFN_CONV - 1, -1),))
        s_states.append(st_s + (ffn_state[n_p:].reshape(bs, FFN_CONV - 1, -1),))
        if l + 1 < depth:
            xpb, xsb = xp.astype(BF16), xs.astype(BF16)
    p_out = [jnp.stack(z) for z in zip(*p_states)]
    s_out = [jnp.stack(z) for z in zip(*s_states)]
    return (xp.reshape(bp, tp, d), xs.reshape(bs, ts, d), *p_out, *s_out)
```

```python
import functools
import math

import numpy as np
import jax
import jax.numpy as jnp
from jax import lax
from jax.experimental import pallas as pl
from jax.experimental.pallas import tpu as pltpu

F32 = jnp.float32
BF16 = jnp.bfloat16

CHUNK = 64
HEAD_DIM = 128
N_KV_A = 4
IDX_HEADS = 16
IDX_DIM = 64
TOPK_MAX = 256
CONV_B = 4
FFN_CONV = 3
N_BUCKETS = 32
MAX_DIST = 128
EPS = 1e-5
IDX_SCALE = (IDX_HEADS ** -0.5) * (IDX_DIM ** -0.5)

LANES = 128
SUBLANES = 8
BF16_ROWS = 16
VMEM_LIMIT = 56 * 1024 * 1024

KEY_BLOCK = LANES
GDN_CHUNK = LANES
INV_BASE = 16
NEG = -1e30
INT_MIN = -2 ** 31

NT_DIMS = (((1,), (1,)), ((), ()))

SM_IK = 0
SM_IW = IDX_DIM
SM_BETA = SM_IW + IDX_HEADS
SM_A = SM_BETA + 16


def _params(n_grid_axes):
    return pltpu.CompilerParams(dimension_semantics=("arbitrary",) * n_grid_axes,
                                vmem_limit_bytes=VMEM_LIMIT)


def _dot(a, b, dims=None):
    a = a.astype(BF16)
    b = b.astype(BF16)
    if dims is None:
        return jnp.dot(a, b, preferred_element_type=F32)
    return lax.dot_general(a, b, dims, preferred_element_type=F32)


def _dot_exact(a, b):
    return jnp.dot(a, b, preferred_element_type=F32, precision=lax.Precision.HIGHEST)


def _silu(x):
    return x * (1.0 / (1.0 + jnp.exp(-x)))


def _ln_kernel(x_ref, g_ref, b_ref, *out_refs):
    x = x_ref[...]
    mu = jnp.mean(x, axis=-1, keepdims=True)
    xc = x - mu
    var = jnp.mean(xc * xc, axis=-1, keepdims=True)
    y = xc * lax.rsqrt(var + EPS) * g_ref[...] + b_ref[...]
    for o_ref in out_refs:
        o_ref[...] = y.astype(o_ref.dtype)


def _layer_norm(x, g, b, out_dtypes):
    m, d = x.shape
    tr = _pick(m, (256, 128, 64))
    row = pl.BlockSpec((tr, d), lambda i: (i, 0))
    vec = pl.BlockSpec((1, d), lambda i: (0, 0))
    outs = pl.pallas_call(
        _ln_kernel,
        out_shape=[jax.ShapeDtypeStruct((m, d), dt) for dt in out_dtypes],
        grid=(m // tr,),
        in_specs=[row, vec, vec],
        out_specs=[row for _ in out_dtypes],
        compiler_params=_params(1),
        name="layer_norm",
    )(x, g.reshape(1, d), b.reshape(1, d))
    return outs


def _mm_kernel(*refs, k_sizes, alpha, has_resid):
    n_lhs = len(k_sizes)
    w_ref = refs[n_lhs]
    o_ref = refs[-1]
    acc = None
    off = 0
    for a_ref, kp in zip(refs[:n_lhs], k_sizes):
        d = jnp.dot(a_ref[...], w_ref[off:off + kp, :], preferred_element_type=F32)
        acc = d if acc is None else acc + d
        off += kp
    if has_resid:
        acc = acc + alpha * refs[n_lhs + 1][...]
    o_ref[...] = acc.astype(o_ref.dtype)


def _matmul(lhs_list, w, *, tm, tn, out_dtype, resid=None, alpha=1.0):
    m = lhs_list[0].shape[0]
    k_sizes = tuple(a.shape[1] for a in lhs_list)
    k, n = w.shape
    assert sum(k_sizes) == k and m % tm == 0 and n % tn == 0
    in_specs = [pl.BlockSpec((tm, kp), lambda i, j: (i, 0)) for kp in k_sizes]
    in_specs.append(pl.BlockSpec((k, tn), lambda i, j: (0, j)))
    args = list(lhs_list) + [w]
    if resid is not None:
        in_specs.append(pl.BlockSpec((tm, tn), lambda i, j: (i, j)))
        args.append(resid)
    return pl.pallas_call(
        functools.partial(_mm_kernel, k_sizes=k_sizes, alpha=alpha, has_resid=resid is not None),
        out_shape=jax.ShapeDtypeStruct((m, n), out_dtype),
        grid=(m // tm, n // tn),
        in_specs=in_specs,
        out_specs=pl.BlockSpec((tm, tn), lambda i, j: (i, j)),
        compiler_params=_params(2),
        name="matmul",
    )(*args)


def _t5_bucket_np(rel):
    half = N_BUCKETS // 2
    max_exact = half // 2
    n = np.abs(rel)
    large = max_exact + np.floor(2.0 * np.log2(np.maximum(n, 1) / max_exact) + 1e-9).astype(np.int64)
    large = np.minimum(large, half - 1)
    return np.where(rel > 0, half, 0) + np.where(n < max_exact, n, large)


def _bias_tiles(rel_bias, tq):
    n_heads = rel_bias.shape[1]
    group = n_heads // N_KV_A
    i = np.arange(tq)[:, None]
    j = np.arange(KEY_BLOCK)[None, :]
    tiles = []
    for r in range(3):
        rel = -r * KEY_BLOCK + j - i
        bucket = _t5_bucket_np(rel)
        if r == 2:
            assert (bucket == N_BUCKETS // 2 - 1).all()
        t = rel_bias[bucket.reshape(-1)].reshape(tq, KEY_BLOCK, N_KV_A, group)
        tiles.append(jnp.transpose(t, (2, 3, 0, 1)).reshape(N_KV_A, group * tq, KEY_BLOCK))
    return jnp.stack(tiles).astype(F32)


def _sortable(x):
    i = lax.bitcast_convert_type(x, jnp.int32)
    return jnp.where(i >= 0, i, i ^ jnp.int32(0x7FFFFFFF))


def _dsa_kernel(q_ref, iq_ref, sm_ref, k_ref, v_ref, ik_ref, bias_ref, o_ref,
                key_scr, mask_scr, logit_scr, m_scr, l_scr, acc_scr, *, tq, qb0, top_k, group):
    qb = pl.program_id(1) + qb0
    nkb = qb + 1
    rows = group * tq

    row = lax.broadcasted_iota(jnp.int32, (tq, KEY_BLOCK), 0)
    lane = lax.broadcasted_iota(jnp.int32, (tq, KEY_BLOCK), 1)
    limit = ((qb * KEY_BLOCK + row) // CHUNK + 1) * CHUNK
    lane_k = lax.broadcasted_iota(jnp.int32, (KEY_BLOCK, LANES), 1)

    iw = sm_ref[:, SM_IW:SM_IW + IDX_HEADS] * IDX_SCALE
    iqb = iq_ref[...].astype(BF16)
    iw_b = [jnp.broadcast_to(iw[:, h:h + 1], (tq, KEY_BLOCK)) for h in range(IDX_HEADS)]

    def idx_body(r, carry):
        kb = qb - r
        ik = ik_ref[pl.ds(pl.multiple_of(kb * KEY_BLOCK, KEY_BLOCK), KEY_BLOCK), :]
        ik_lo = jnp.where(lane_k < IDX_DIM, ik, 0.0)
        ik_hi = pltpu.roll(ik_lo, IDX_DIM, axis=1)
        rhs = jnp.concatenate([ik_lo, ik_hi], axis=0).astype(BF16)
        acc = jnp.zeros((tq, KEY_BLOCK), F32)
        for g in range(IDX_HEADS // 2):
            s = lax.dot_general(iqb[:, g * LANES:(g + 1) * LANES], rhs, NT_DIMS,
                                preferred_element_type=F32)
            acc = acc + jnp.maximum(s[:, :KEY_BLOCK], 0.0) * iw_b[2 * g]
            acc = acc + jnp.maximum(s[:, KEY_BLOCK:], 0.0) * iw_b[2 * g + 1]
        admissible = kb * KEY_BLOCK + lane < limit
        key_scr[r] = jnp.where(admissible, _sortable(acc), INT_MIN)
        return carry

    lax.fori_loop(0, nkb, idx_body, 0)

    def count_ge(cand):
        cand_b = jnp.broadcast_to(cand, (tq, KEY_BLOCK))

        def body(r, c):
            return c + jnp.where(key_scr[r] >= cand_b, 1.0, 0.0)

        c = lax.fori_loop(0, nkb, body, jnp.zeros((tq, KEY_BLOCK), F32))
        return jnp.sum(c, axis=1, keepdims=True)

    kf = float(top_k)
    zero = jnp.zeros((tq, 1), jnp.int32)
    thr = jnp.where(count_ge(zero) >= kf, zero, jnp.int32(INT_MIN))

    def bit_body(it, thr):
        cand = thr | jnp.left_shift(jnp.int32(1), 30 - it)
        return jnp.where(count_ge(cand) >= kf, cand, thr)

    thr = lax.fori_loop(0, 31, bit_body, thr)
    thr_b = jnp.broadcast_to(thr, (tq, KEY_BLOCK))

    def mask_body(r, carry):
        key = key_scr[r]
        sel = (key >= thr_b) & (key != INT_MIN)
        m = jnp.where(sel, 0.0, NEG)
        mask_scr[r] = jnp.concatenate([m] * group, axis=0)
        return carry

    lax.fori_loop(0, nkb, mask_body, 0)

    scale = HEAD_DIM ** -0.5
    for n in range(N_KV_A):
        q4 = jnp.concatenate(
            [q_ref[:, (group * n + g) * HEAD_DIM:(group * n + g + 1) * HEAD_DIM] for g in range(group)],
            axis=0).astype(BF16)
        m_scr[...] = jnp.full((rows, KEY_BLOCK), NEG, F32)

        def kv_block(ref, r):
            kb = qb - r
            return ref[pl.ds(pl.multiple_of(kb * KEY_BLOCK, KEY_BLOCK), KEY_BLOCK),
                       n * HEAD_DIM:(n + 1) * HEAD_DIM].astype(BF16)

        def logits_body(it, carry):
            rs = [jnp.minimum(2 * it + u, nkb - 1) for u in range(2)]
            ks = [kv_block(k_ref, r) for r in rs]
            ss = [lax.dot_general(q4, kblk, NT_DIMS, preferred_element_type=F32) for kblk in ks]
            ss = [s * scale + bias_ref[jnp.minimum(r, 2), n] + mask_scr[r] for s, r in zip(ss, rs)]
            for s, r in zip(ss, rs):
                logit_scr[r] = s
            m_scr[...] = jnp.maximum(m_scr[...], jnp.maximum(ss[0], ss[1]))
            return carry

        lax.fori_loop(0, (nkb + 1) // 2, logits_body, 0)
        m = jnp.max(m_scr[...], axis=1, keepdims=True)
        l_scr[...] = jnp.zeros((rows, KEY_BLOCK), F32)
        acc_scr[...] = jnp.zeros((rows, HEAD_DIM), F32)

        def pv_body(it, carry):
            rs = [jnp.minimum(2 * it + u, nkb - 1) for u in range(2)]
            live = [1.0, jnp.where(2 * it + 1 < nkb, 1.0, 0.0)]
            ps = [jnp.exp(logit_scr[r] - m) * w for r, w in zip(rs, live)]
            vs = [kv_block(v_ref, r) for r in rs]
            pv = [jnp.dot(p.astype(BF16), vblk, preferred_element_type=F32) for p, vblk in zip(ps, vs)]
            l_scr[...] += ps[0] + ps[1]
            acc_scr[...] += pv[0] + pv[1]
            return carry

        lax.fori_loop(0, (nkb + 1) // 2, pv_body, 0)
        out = acc_scr[...] / jnp.sum(l_scr[...], axis=1, keepdims=True)
        for g in range(group):
            h = group * n + g
            o_ref[:, h * HEAD_DIM:(h + 1) * HEAD_DIM] = out[g * tq:(g + 1) * tq].astype(o_ref.dtype)


def _dsa(q_src, q_blk, iq_src, iq_blk, small, k_src, k_blk, v_src, v_blk, ik_src, bias, *,
         t, tq, qb0, top_k, n_heads):
    b = q_src.shape[0]
    l = k_src.shape[1]
    group = n_heads // N_KV_A
    nq = t // tq
    nkb_max = qb0 + nq
    assert l == nkb_max * KEY_BLOCK and t % tq == 0 and tq % CHUNK == 0 and tq <= KEY_BLOCK
    assert tq == KEY_BLOCK or nq == 1
    w_a = n_heads * HEAD_DIM
    w_kv = N_KV_A * HEAD_DIM
    w_iq = IDX_HEADS * IDX_DIM
    rows = group * tq
    kern = functools.partial(_dsa_kernel, tq=tq, qb0=qb0, top_k=top_k, group=group)
    return pl.pallas_call(
        kern,
        out_shape=jax.ShapeDtypeStruct((b, t, w_a), BF16),
        grid=(b, nq),
        in_specs=[
            pl.BlockSpec((None, tq, w_a), lambda bi, i: (bi, i, q_blk)),
            pl.BlockSpec((None, tq, w_iq), lambda bi, i: (bi, i, iq_blk)),
            pl.BlockSpec((None, tq, LANES), lambda bi, i: (bi, i, 0)),
            pl.BlockSpec((None, l, w_kv), lambda bi, i: (bi, 0, k_blk)),
            pl.BlockSpec((None, l, w_kv), lambda bi, i: (bi, 0, v_blk)),
            pl.BlockSpec((None, l, LANES), lambda bi, i: (bi, 0, 0)),
            pl.BlockSpec((3, N_KV_A, rows, KEY_BLOCK), lambda bi, i: (0, 0, 0, 0)),
        ],
        out_specs=pl.BlockSpec((None, tq, w_a), lambda bi, i: (bi, i, 0)),
        scratch_shapes=[
            pltpu.VMEM((nkb_max, tq, KEY_BLOCK), jnp.int32),
            pltpu.VMEM((nkb_max, rows, KEY_BLOCK), F32),
            pltpu.VMEM((nkb_max, rows, KEY_BLOCK), F32),
            pltpu.VMEM((rows, KEY_BLOCK), F32),
            pltpu.VMEM((rows, KEY_BLOCK), F32),
            pltpu.VMEM((rows, HEAD_DIM), F32),
        ],
        compiler_params=_params(2),
        name="dsa",
    )(q_src, iq_src, small, k_src, v_src, ik_src, bias)


def _gdn_kernel(xq_ref, xk_ref, xv_ref, z_ref, sm_ref, pq_ref, pk_ref, pv_ref,
                wq_ref, wk_ref, wv_ref, bq_ref, bk_ref, bv_ref, alog_ref, dtb_ref, ng_ref, s0_ref,
                o_ref, s_ref, gc_scr, gt_scr, u_scr, w_scr, qd_scr, kdt_scr, qk_scr, gl_scr, *, t, hb, c):
    hg = pl.program_id(1)
    n_chunks = t // c

    ri = lax.broadcasted_iota(jnp.int32, (c, c), 0)
    ci = lax.broadcasted_iota(jnp.int32, (c, c), 1)
    tri = ri >= ci
    strict = ri > ci
    tril = jnp.where(tri, 1.0, 0.0)
    eye = jnp.where(ri == ci, 1.0, 0.0)
    level_masks = [ri // INV_BASE == ci // INV_BASE]
    size = INV_BASE
    while size < c:
        level_masks.append((ri // (2 * size) == ci // (2 * size)) & (ri // size != ci // size))
        size *= 2
    a_scale = -jnp.exp(alog_ref[...])
    dtb = dtb_ref[...]
    for bi in range(n_chunks):
        a = sm_ref[bi * c:(bi + 1) * c, :] + dtb
        g = a_scale * (jnp.maximum(a, 0.0) + jnp.log1p(jnp.exp(-jnp.abs(a))))
        gc = _dot_exact(tril, g)
        gc_scr[bi * c:(bi + 1) * c, :] = gc
        if c < LANES:
            gc = jnp.concatenate([gc, jnp.zeros((LANES - c, LANES), F32)], axis=0)
        gt_scr[bi] = gc.T[:, :c]

    lane = lax.broadcasted_iota(jnp.int32, (c, LANES), 1)
    ng = ng_ref[...]

    def conv(win, w_ref, b_ref, hh):
        cols = slice(hh * HEAD_DIM, (hh + 1) * HEAD_DIM)
        y = b_ref[:, cols]
        for j in range(CONV_B):
            lo = SUBLANES - (CONV_B - 1) + j
            y = y + win[lo:lo + c, cols] * w_ref[j:j + 1, cols]
        return _silu(y)

    def l2n(x):
        return x * lax.rsqrt(jnp.sum(x * x, axis=-1, keepdims=True) + 1e-6)

    def window(x_ref, p_ref, ck, row0):
        halo = x_ref[pl.ds(pl.multiple_of(jnp.maximum(row0 - SUBLANES, 0), SUBLANES), SUBLANES), :]
        halo = jnp.where(ck == 0, p_ref[...], halo)
        return jnp.concatenate([halo, x_ref[pl.ds(row0, c), :]], axis=0)

    cpi = 2 if n_chunks % 2 == 0 else 1

    def prep(it, carry):
        chains = []
        for sub in range(cpi):
            ck = it * cpi + sub
            row0 = pl.multiple_of(ck * c, c)
            wq = window(xq_ref, pq_ref, ck, row0)
            wk = window(xk_ref, pk_ref, ck, row0)
            wv = window(xv_ref, pv_ref, ck, row0)
            beta_all = 1.0 / (1.0 + jnp.exp(-sm_ref[pl.ds(row0, c), :]))
            gc_all = gc_scr[pl.ds(row0, c), :]
            for hh in range(hb):
                h = hg * hb + hh
                ch = {"ck": ck, "row0": row0, "hh": hh}
                ch["beta"] = jnp.sum(jnp.where(lane == SM_BETA + h, beta_all, 0.0), axis=1, keepdims=True)
                gcol = jnp.sum(jnp.where(lane == SM_A + h, gc_all, 0.0), axis=1, keepdims=True)
                grow = gt_scr[ck, pl.ds(SM_A + h, 1), :]
                ch["glast"] = gcol[c - 1:c, :]
                ch["gcol"] = gcol
                ch["dmask"] = jnp.where(tri, jnp.exp(jnp.where(tri, gcol - grow, 0.0)), 0.0)
                ch["q"] = l2n(conv(wq, wq_ref, bq_ref, hh)) * (HEAD_DIM ** -0.5)
                ch["k"] = l2n(conv(wk, wk_ref, bk_ref, hh))
                ch["v"] = conv(wv, wv_ref, bv_ref, hh)
                ch["kb"] = ch["k"] * ch["beta"]
                chains.append(ch)
        for ch in chains:
            ch["kk"] = _dot(ch["kb"], ch["k"], NT_DIMS)
            ch["qk"] = _dot(ch["q"], ch["k"], NT_DIMS)
        for ch in chains:
            ch["lmat"] = jnp.where(strict, ch["kk"] * ch["dmask"], 0.0)
            ch["npow"] = jnp.where(level_masks[0], -ch["lmat"], 0.0)
            ch["ainv"] = eye + ch["npow"]
        for _ in range(int(math.log2(INV_BASE)) - 1):
            for ch in chains:
                ch["npow"] = _dot(ch["npow"], ch["npow"])
            for ch in chains:
                ch["ainv"] = ch["ainv"] + _dot(ch["ainv"], ch["npow"])
        for lm in level_masks[1:]:
            for ch in chains:
                ch["tmp"] = _dot(ch["ainv"], jnp.where(lm, ch["lmat"], 0.0))
            for ch in chains:
                ch["ainv"] = ch["ainv"] - _dot(ch["tmp"], ch["ainv"])
        for ch in chains:
            ch["eg"] = jnp.exp(ch["gcol"])
            ch["sol"] = _dot(ch["ainv"], jnp.concatenate([ch["v"] * ch["beta"], ch["kb"] * ch["eg"]], axis=1))
        for ch in chains:
            ck, row0, hh = ch["ck"], ch["row0"], ch["hh"]
            qk = jnp.where(tri, ch["qk"] * ch["dmask"], 0.0)
            k_dec = ch["k"] * jnp.exp(ch["glast"] - ch["gcol"])
            u_scr[hh, pl.ds(row0, c), :] = ch["sol"][:, :HEAD_DIM]
            w_scr[hh, pl.ds(row0, c), :] = ch["sol"][:, HEAD_DIM:].astype(BF16)
            qd_scr[hh, pl.ds(row0, c), :] = (ch["q"] * ch["eg"]).astype(BF16)
            kdt_scr[hh, ck] = k_dec.T.astype(BF16) if c == LANES else jnp.concatenate(
                [k_dec, jnp.zeros((LANES - c, HEAD_DIM), F32)], axis=0).T[:, :c].astype(BF16)
            qk_scr[hh, ck] = qk.astype(BF16)
            gl_scr[hh, pl.ds(ck, 1), :] = jnp.broadcast_to(jnp.exp(ch["glast"]), (1, LANES))
        return carry

    lax.fori_loop(0, n_chunks // cpi, prep, 0)

    def scan(ck, states):
        row0 = pl.multiple_of(ck * c, c)
        new_states = []
        for hh in range(hb):
            cols = slice(hh * HEAD_DIM, (hh + 1) * HEAD_DIM)
            s = states[hh]
            sb = s.astype(BF16)
            v_new = u_scr[hh, pl.ds(row0, c), :] - jnp.dot(w_scr[hh, pl.ds(row0, c), :], sb,
                                                         preferred_element_type=F32)
            vb = v_new.astype(BF16)
            o = jnp.dot(qd_scr[hh, pl.ds(row0, c), :], sb, preferred_element_type=F32)
            o = o + jnp.dot(qk_scr[hh, ck], vb, preferred_element_type=F32)
            s = s * gl_scr[hh, pl.ds(ck, 1), :] + jnp.dot(kdt_scr[hh, ck], vb, preferred_element_type=F32)
            new_states.append(s)
            o = o * lax.rsqrt(jnp.mean(o * o, axis=-1, keepdims=True) + EPS) * ng
            o = o * _silu(z_ref[pl.ds(row0, c), cols])
            o_ref[pl.ds(row0, c), cols] = o.astype(o_ref.dtype)
        return tuple(new_states)

    states = lax.fori_loop(0, n_chunks, scan, tuple(s0_ref[hh] for hh in range(hb)))
    for hh in range(hb):
        s_ref[hh] = states[hh]


def _gdn(h_main, qkv_col0, z_col0, small, prev8, conv_w, conv_b, alog_row, dtb_row, norm_g, s0, *, hb):
    b, t, _ = h_main.shape
    n_heads = s0.shape[1]
    w_b = n_heads * HEAD_DIM
    hw = hb * HEAD_DIM
    ng = n_heads // hb
    c = GDN_CHUNK if t % GDN_CHUNK == 0 else CHUNK
    n_chunks = t // c
    assert t % c == 0 and n_heads % hb == 0

    def col(c0):
        return lambda bi, g: (bi, 0, c0 // hw + g)

    def wcol(c0):
        return lambda bi, g: (0, c0 // hw + g)

    tok = lambda c0: pl.BlockSpec((None, t, hw), col(c0))
    prev = lambda c0: pl.BlockSpec((None, SUBLANES, hw), col(c0))
    cw = lambda c0: pl.BlockSpec((CONV_B, hw), wcol(c0))
    cb = lambda c0: pl.BlockSpec((1, hw), wcol(c0))
    vec = pl.BlockSpec((1, LANES), lambda bi, g: (0, 0))
    state = pl.BlockSpec((None, hb, HEAD_DIM, HEAD_DIM), lambda bi, g: (bi, g, 0, 0))
    return pl.pallas_call(
        functools.partial(_gdn_kernel, t=t, hb=hb, c=c),
        out_shape=[jax.ShapeDtypeStruct((b, t, w_b), BF16),
                   jax.ShapeDtypeStruct(s0.shape, F32)],
        grid=(b, ng),
        in_specs=[tok(qkv_col0), tok(qkv_col0 + w_b), tok(qkv_col0 + 2 * w_b), tok(z_col0),
                  pl.BlockSpec((None, t, LANES), lambda bi, g: (bi, 0, 0)),
                  prev(0), prev(w_b), prev(2 * w_b),
                  cw(0), cw(w_b), cw(2 * w_b), cb(0), cb(w_b), cb(2 * w_b),
                  vec, vec, vec, state],
        out_specs=[pl.BlockSpec((None, t, hw), lambda bi, g: (bi, 0, g)), state],
        scratch_shapes=[pltpu.VMEM((t, LANES), F32),
                        pltpu.VMEM((n_chunks, LANES, c), F32),
                        pltpu.VMEM((hb, t, HEAD_DIM), F32),
                        pltpu.VMEM((hb, t, HEAD_DIM), BF16),
                        pltpu.VMEM((hb, t, HEAD_DIM), BF16),
                        pltpu.VMEM((hb, n_chunks, HEAD_DIM, c), BF16),
                        pltpu.VMEM((hb, n_chunks, c, c), BF16),
                        pltpu.VMEM((hb, max(n_chunks, SUBLANES), LANES), F32)],
        compiler_params=_params(2),
        name="gdn",
    )(h_main, h_main, h_main, h_main, small, prev8, prev8, prev8,
      conv_w, conv_w, conv_w, conv_b, conv_b, conv_b, alog_row, dtb_row, norm_g, s0)


def _ffn_up_kernel(*refs, tm, seg, has_prev):
    x_ref, wg_ref, wv_ref, cwg_ref, cwv_ref, cbg_ref, cbv_ref = refs[:7]
    n_in = 11 if has_prev else 7
    o_ref = refs[n_in]
    ext_scrs = refs[n_in + 1:n_in + 3]
    carry_scr = refs[n_in + 3]
    i = pl.program_id(0)
    j = pl.program_id(1)
    tn = o_ref.shape[1]
    nseg = tm // seg if has_prev else 0

    @pl.when(i == 0)
    def _():
        carry_scr[0, j] = jnp.zeros((SUBLANES, tn), F32)
        carry_scr[1, j] = jnp.zeros((SUBLANES, tn), F32)

    branches = ((0, wg_ref, cwg_ref, cbg_ref, refs[7:9]), (1, wv_ref, cwv_ref, cbv_ref, refs[9:11]))
    n_part = 2 if tm % (2 * 256) == 0 else 1
    rp = tm // n_part

    def matmuls(c):
        for slot, w_ref, _, _, _ in branches:
            up = jnp.dot(x_ref[c * rp:(c + 1) * rp, :], w_ref[...], preferred_element_type=F32)
            ext_scrs[slot][SUBLANES + c * rp:SUBLANES + (c + 1) * rp, :] = up

    def conv_gate(c):
        rows = slice(c * rp, (c + 1) * rp)
        rr = (i * tm + c * rp + lax.broadcasted_iota(jnp.int32, (rp, tn), 0)) % seg
        if has_prev:
            srow = c * rp + lax.broadcasted_iota(jnp.int32, (rp, 2 * nseg), 0)
            scol = lax.broadcasted_iota(jnp.int32, (rp, 2 * nseg), 1)
            place = jnp.where(srow == (scol % nseg) * seg + scol // nseg, 1.0, 0.0).astype(BF16)
        ys = []
        for slot, _, cw_ref, cb_ref, prev_refs in branches:
            ext = ext_scrs[slot]
            lo = SUBLANES + c * rp
            y = cb_ref[...] + ext[lo:lo + rp, :] * cw_ref[2:3, :]
            y = y + jnp.where(rr >= 1, ext[lo - 1:lo - 1 + rp, :], 0.0) * cw_ref[1:2, :]
            y = y + jnp.where(rr >= 2, ext[lo - 2:lo - 2 + rp, :], 0.0) * cw_ref[0:1, :]
            if has_prev:
                p0, p1 = prev_refs[0][...], prev_refs[1][...]
                corr = jnp.concatenate([p0 * cw_ref[0:1, :] + p1 * cw_ref[1:2, :], p1 * cw_ref[0:1, :]], axis=0)
                for _ in range(3):
                    part = corr.astype(BF16)
                    y = y + jnp.dot(place, part, preferred_element_type=F32)
                    corr = corr - part.astype(F32)
            ys.append(y)
        o_ref[rows, :] = (_silu(ys[0]) * ys[1]).astype(o_ref.dtype)

    for slot in range(2):
        ext_scrs[slot][0:SUBLANES, :] = carry_scr[slot, j]
    matmuls(0)
    for c in range(1, n_part):
        matmuls(c)
        conv_gate(c - 1)
    conv_gate(n_part - 1)
    for slot in range(2):
        carry_scr[slot, j] = ext_scrs[slot][tm:tm + SUBLANES, :]


def _ffn_up(x1b, w_up, conv_w, conv_b, prev, *, seg, tm, tn):
    m, d = x1b.shape
    f = w_up.shape[1] // 2
    nj = f // tn
    has_prev = prev is not None
    assert m % tm == 0 and f % tn == 0 and (seg % tm == 0 or tm % seg == 0) and tm % BF16_ROWS == 0
    in_specs = [
        pl.BlockSpec((tm, d), lambda i, j: (i, 0)),
        pl.BlockSpec((d, tn), lambda i, j: (0, j)),
        pl.BlockSpec((d, tn), lambda i, j: (0, nj + j)),
        pl.BlockSpec((FFN_CONV, tn), lambda i, j: (0, j)),
        pl.BlockSpec((FFN_CONV, tn), lambda i, j: (0, nj + j)),
        pl.BlockSpec((1, tn), lambda i, j: (0, j)),
        pl.BlockSpec((1, tn), lambda i, j: (0, nj + j)),
    ]
    args = [x1b, w_up, w_up, conv_w, conv_w, conv_b, conv_b]
    if has_prev:
        nseg = tm // seg
        p0, p1 = prev[:, 0, :], prev[:, 1, :]
        in_specs += [pl.BlockSpec((nseg, tn), lambda i, j: (i, j)),
                     pl.BlockSpec((nseg, tn), lambda i, j: (i, j)),
                     pl.BlockSpec((nseg, tn), lambda i, j: (i, nj + j)),
                     pl.BlockSpec((nseg, tn), lambda i, j: (i, nj + j))]
        args += [p0, p1, p0, p1]
    return pl.pallas_call(
        functools.partial(_ffn_up_kernel, tm=tm, seg=seg, has_prev=has_prev),
        out_shape=jax.ShapeDtypeStruct((m, f), BF16),
        grid=(m // tm, nj),
        in_specs=in_specs,
        out_specs=pl.BlockSpec((tm, tn), lambda i, j: (i, j)),
        scratch_shapes=[pltpu.VMEM((SUBLANES + tm, tn), F32),
                        pltpu.VMEM((SUBLANES + tm, tn), F32),
                        pltpu.VMEM((2, nj, SUBLANES, tn), F32)],
        compiler_params=_params(2),
        name="ffn_up",
    )(*args)


def _pick(n, candidates):
    for c in candidates:
        if n % c == 0:
            return c
    raise ValueError(f"no tile for {n}")


def _layer(xn, xnb, cache, w, *, b, t):
    m, d = xn.shape
    n_heads = w["n_heads"]
    w_a = n_heads * HEAD_DIM
    w_kv = N_KV_A * HEAD_DIM
    w_iq = IDX_HEADS * IDX_DIM
    c_qkv = 3 * w_a
    tm = _pick(m, (1024, 512, 256, 128, 64))

    h_main = _matmul([xnb], w["w_main"], tm=tm, tn=512, out_dtype=F32)
    small = _matmul([xnb], w["w_small"], tm=tm, tn=LANES, out_dtype=F32)
    h3 = h_main.reshape(b, t, -1)
    small3 = small.reshape(b, t, LANES)
    col_k, col_v, col_iq, col_qkv, col_z = w_a, w_a + w_kv, w_a + 2 * w_kv, w_a + 2 * w_kv + w_iq, None
    col_z = col_qkv + c_qkv
    k_new = h3[:, :, col_k:col_k + w_kv]
    v_new = h3[:, :, col_v:col_v + w_kv]
    ik_new = small3[:, :, SM_IK:SM_IK + IDX_DIM]
    qkv_b = h3[:, :, col_qkv:col_qkv + c_qkv]

    if cache is None:
        past = 0
        tq = KEY_BLOCK
        assert t % KEY_BLOCK == 0
        k_src, k_blk, v_src, v_blk, ik_src = h3, col_k // w_kv, h3, col_v // w_kv, small3
    else:
        past = cache["k"].shape[1]
        tq = t
        assert past % KEY_BLOCK == 0 and t <= KEY_BLOCK
        pad = KEY_BLOCK - t
        k_src = jnp.concatenate([cache["k"].reshape(b, past, w_kv), k_new,
                                 jnp.zeros((b, pad, w_kv), F32)], axis=1)
        v_src = jnp.concatenate([cache["v"].reshape(b, past, w_kv), v_new,
                                 jnp.zeros((b, pad, w_kv), F32)], axis=1)
        ik_all = jnp.concatenate([cache["ik"], ik_new, jnp.zeros((b, pad, IDX_DIM), F32)], axis=1)
        ik_src = jnp.pad(ik_all, ((0, 0), (0, 0), (0, LANES - IDX_DIM)))
        k_blk = v_blk = 0
    top_k = min(TOPK_MAX, (past + t) // 4)
    out_a = _dsa(h3, 0, h3, col_iq // w_iq, small3, k_src, k_blk, v_src, v_blk, ik_src,
                 _bias_tiles(w["rel_bias"], tq), t=t, tq=tq, qb0=past // KEY_BLOCK, top_k=top_k,
                 n_heads=n_heads)

    if cache is None:
        prev8 = jnp.zeros((b, SUBLANES, c_qkv), F32)
        s0 = jnp.zeros((b, n_heads, HEAD_DIM, HEAD_DIM), F32)
    else:
        prev8 = jnp.pad(cache["conv"], ((0, 0), (SUBLANES - (CONV_B - 1), 0), (0, 0)))
        s0 = cache["delta"]
    o_b, s_new = _gdn(h3, col_qkv, col_z, small3, prev8, w["conv_qkv_w"], w["conv_qkv_b"],
                      w["alog_row"], w["dtb_row"], w["norm_g"], s0, hb=2 if cache is None else 8)
    if t >= CONV_B - 1:
        conv_state = qkv_b[:, t - (CONV_B - 1):, :]
    else:
        conv_state = jnp.concatenate([cache["conv"], qkv_b], axis=1)[:, -(CONV_B - 1):, :]

    tm2 = _pick(m, (512, 256, 128, 64))
    pre1 = _matmul([out_a.reshape(m, w_a), o_b.reshape(m, w_a)], w["w_o"], tm=tm, tn=512,
                   out_dtype=F32, resid=xn, alpha=w["alpha"])
    x1, x1b = _layer_norm(pre1, w["ln1_g"], w["ln1_b"], (F32, BF16))

    f2 = w["w_ffn_up"].shape[1]
    tn_f = _pick(f2 // 2, (256, 128))
    if cache is None:
        act = _ffn_up(x1b, w["w_ffn_up"], w["ffn_conv_w"], w["ffn_conv_b"], None, seg=t, tm=min(tm, t), tn=tn_f)
    else:
        tm_s = _pick(m, (1024, 512)) if m >= 512 else m
        assert tm_s % t == 0 and (tm_s // t) % SUBLANES == 0 or tm_s == m
        act = _ffn_up(x1b, w["w_ffn_up"], w["ffn_conv_w"], w["ffn_conv_b"], cache["ffn"], seg=t, tm=tm_s,
                      tn=tn_f)
    pre2 = _matmul([act], w["w_ffn_down"], tm=tm2, tn=256, out_dtype=F32, resid=x1, alpha=w["alpha"])
    (y,) = _layer_norm(pre2, w["ln2_g"], w["ln2_b"], (F32,))

    x1b3 = x1b.reshape(b, t, d)
    tail = x1b3[:, t - (FFN_CONV - 1):, :].reshape(b * (FFN_CONV - 1), d)
    states = (k_new.reshape(b, t, N_KV_A, HEAD_DIM), v_new.reshape(b, t, N_KV_A, HEAD_DIM), ik_new,
              s_new, conv_state)
    return y, x1, states, tail


def _prep_weights(l, depth, w_in, conv_qkv_w, conv_qkv_b, a_log, dt_bias, delta_norm_g, rel_bias, w_o,
                  ln1_g, ln1_b, w_ffn_up, ffn_conv_w, ffn_conv_b, w_ffn_down, ln2_g, ln2_b):
    d = w_in.shape[1]
    n_heads = d // (2 * HEAD_DIM)
    w_a = n_heads * HEAD_DIM
    w_kv = N_KV_A * HEAD_DIM
    w_iq = IDX_HEADS * IDX_DIM
    c_qkv = 3 * w_a
    sizes = (w_a, w_kv, w_kv, w_iq, IDX_DIM, IDX_HEADS, c_qkv, w_a, n_heads, n_heads)
    offs = np.concatenate([[0], np.cumsum(sizes)])
    assert offs[-1] == w_in.shape[2] and n_heads == 16
    wi = w_in[l]
    seg = lambda a, b_: wi[:, offs[a]:offs[b_]]
    w_main = jnp.concatenate([seg(0, 4), seg(6, 8)], axis=1).astype(BF16)
    pad = LANES - (IDX_DIM + IDX_HEADS + 2 * n_heads)
    w_small = jnp.concatenate([seg(4, 6), seg(8, 10), jnp.zeros((d, pad), F32)], axis=1).astype(BF16)
    lane_pad = lambda v: jnp.pad(v.astype(F32), (SM_A, LANES - SM_A - n_heads)).reshape(1, LANES)
    return {
        "n_heads": n_heads,
        "alpha": float((2 * depth) ** 0.25),
        "w_main": w_main, "w_small": w_small,
        "conv_qkv_w": conv_qkv_w[l], "conv_qkv_b": conv_qkv_b[l].reshape(1, -1),
        "alog_row": lane_pad(a_log[l]), "dtb_row": lane_pad(dt_bias[l]),
        "norm_g": delta_norm_g[l].reshape(1, -1),
        "rel_bias": rel_bias,
        "w_o": w_o[l].astype(BF16),
        "ln1_g": ln1_g[l], "ln1_b": ln1_b[l],
        "w_ffn_up": w_ffn_up[l].astype(BF16),
        "ffn_conv_w": ffn_conv_w[l], "ffn_conv_b": ffn_conv_b[l].reshape(1, -1),
        "w_ffn_down": w_ffn_down[l].astype(BF16),
        "ln2_g": ln2_g[l], "ln2_b": ln2_b[l],
    }


def kernel(x_prompt, x_sample, cache_attn_k, cache_attn_v, cache_idx_k, state_delta, state_conv_qkv,
           state_ffn_conv, ln_in_g, ln_in_b, w_in, conv_qkv_w, conv_qkv_b, a_log, dt_bias, delta_norm_g,
           rel_bias, w_o, ln1_g, ln1_b, w_ffn_up, ffn_conv_w, ffn_conv_b, w_ffn_down, ln2_g, ln2_b):
    bp, tp, d = x_prompt.shape
    bs, ts, _ = x_sample.shape
    depth = w_in.shape[0]
    xp, xpb = _layer_norm(x_prompt.reshape(bp * tp, d), ln_in_g, ln_in_b, (F32, BF16))
    xs, xsb = _layer_norm(x_sample.reshape(bs * ts, d), ln_in_g, ln_in_b, (F32, BF16))
    p_states, s_states = [], []
    for l in range(depth):
        w = _prep_weights(l, depth, w_in, conv_qkv_w, conv_qkv_b, a_log, dt_bias, delta_norm_g, rel_bias,
                          w_o, ln1_g, ln1_b, w_ffn_up, ffn_conv_w, ffn_conv_b, w_ffn_down, ln2_g, ln2_b)
        cache = {"k": cache_attn_k[l], "v": cache_attn_v[l], "ik": cache_idx_k[l],
                 "delta": state_delta[l], "conv": state_conv_qkv[l], "ffn": state_ffn_conv[l]}
        xp, xp1, st_p, tail_p = _layer(xp, xpb, None, w, b=bp, t=tp)
        xs, xs1, st_s, tail_s = _layer(xs, xsb, cache, w, b=bs, t=ts)
        tail = jnp.concatenate([tail_p, tail_s], axis=0)
        ffn_state = _matmul([tail], w["w_ffn_up"], tm=tail.shape[0], tn=_pick(w["w_ffn_up"].shape[1], (512, 256, 128)),
                            out_dtype=F32)
        n_p = bp * (FFN_CONV - 1)
        p_states.append(st_p + (ffn_state[:n_p].reshape(bp, FFN_CONV - 1, -1),))
        s_states.append(st_s + (ffn_state[n_p:].reshape(bs, FFN_CONV - 1, -1),))
        if l + 1 < depth:
            xpb, xsb = xp.astype(BF16), xs.astype(BF16)
    p_out = [jnp.stack(z) for z in zip(*p_states)]
    s_out = [jnp.stack(z) for z in zip(*s_states)]
    return (xp.reshape(bp, tp, d), xs.reshape(bs, ts, d), *p_out, *s_out)
```
